```python
import jax, jax.numpy as jnp
from jax import lax
import numpy as np

D_MODEL = 1024
BATCH = 16
SEQ = 4096
DEPTH = 4

MEM_LEN = 256
HEAD_DIM = 64
CONV_WIDTH = D_MODEL // 2
CONV_K = 31
DSA_HEADS = 8
DSA_WIDTH = DSA_HEADS * HEAD_DIM
IDX_HEADS = 4
IDX_DIM = 64
TOPK_MAX = 256
Q_BLOCK = 128
SG_CHUNK = 128
SG_GROUPS = 8
SG_WIDTH = D_MODEL
MEM_HEADS = 4
MEM_HEAD_DIM = 128
MEM_WIDTH = MEM_HEADS * MEM_HEAD_DIM

ROPE_THETA = 10000.0
LN_EPS = 1e-5
DN_ALPHA = (2 * DEPTH) ** 0.25
DN_BETA = (8 * DEPTH) ** -0.25

EVEN_SPLITS = (CONV_WIDTH, CONV_WIDTH, CONV_WIDTH,
               DSA_WIDTH, HEAD_DIM, HEAD_DIM,
               IDX_HEADS * IDX_DIM, IDX_DIM, IDX_HEADS, DSA_WIDTH,
               MEM_WIDTH, MEM_WIDTH)
ODD_SPLITS = (SG_WIDTH, SG_WIDTH, SG_WIDTH,
              MEM_WIDTH, MEM_WIDTH)
EVEN_IN = sum(EVEN_SPLITS)
ODD_IN = sum(ODD_SPLITS)
EVEN_OUT = CONV_WIDTH + DSA_WIDTH + MEM_WIDTH
ODD_OUT = SG_WIDTH + MEM_WIDTH
N_EVEN = (DEPTH + 1) // 2
N_ODD = DEPTH // 2

kernel_name = "hybrid_conv_dsa_sgmlp_memory_deepnorm"


def _split(h, sizes):
    return jnp.split(h, tuple(int(c) for c in np.cumsum(sizes)[:-1]), axis=-1)


def layer_norm(x, g, b):
    xf = x.astype(jnp.float32)
    mu = jnp.mean(xf, -1, keepdims=True)
    var = jnp.mean(jnp.square(xf - mu), -1, keepdims=True)
    y = (xf - mu) * lax.rsqrt(var + LN_EPS) * g.astype(jnp.float32) + b.astype(jnp.float32)
    return y.astype(x.dtype)


def rope_tables(positions, dim):
    inv = ROPE_THETA ** (-jnp.arange(0, dim, 2, dtype=jnp.float32) / dim)
    ang = positions.astype(jnp.float32)[..., None] * inv
    return jnp.cos(ang), jnp.sin(ang)


def apply_rope(t, cos, sin):
    c = cos[:, :, None, :].astype(t.dtype)
    s = sin[:, :, None, :].astype(t.dtype)
    t1, t2 = jnp.split(t, 2, axis=-1)
    return jnp.concatenate([t1 * c - t2 * s, t2 * c + t1 * s], axis=-1)


def conformer_conv(a_val, a_glu, conv_w, conv_b, ln_g, ln_b, pw2_w, pw2_b):
    h = a_val * jax.nn.sigmoid(a_glu)
    h = lax.conv_general_dilated(h, conv_w[:, None, :].astype(h.dtype), window_strides=(1,),
                                 padding=[(CONV_K - 1, 0)],
                                 dimension_numbers=("NWC", "WIO", "NWC"),
                                 feature_group_count=CONV_WIDTH) + conv_b
    h = jax.nn.silu(layer_norm(h, ln_g, ln_b))
    return h @ pw2_w + pw2_b


def dsa_attention(q, kv, q_idx, k_idx, w_idx):
    B, S = q.shape[0], q.shape[1]
    topk = min(TOPK_MAX, S // 4)
    key_pos = jnp.arange(S)

    def block(i):
        start = i * Q_BLOCK
        qb = lax.dynamic_slice_in_dim(q, start, Q_BLOCK, 1)
        qib = lax.dynamic_slice_in_dim(q_idx, start, Q_BLOCK, 1)
        wb = lax.dynamic_slice_in_dim(w_idx, start, Q_BLOCK, 1)
        qpos = start + jnp.arange(Q_BLOCK)
        logits = jnp.einsum("bthd,bsd->bths", qib, k_idx, preferred_element_type=jnp.float32)
        score = jnp.einsum("bths,bth->bts", jax.nn.relu(logits), wb.astype(jnp.float32))
        causal = key_pos[None, :] <= qpos[:, None]
        score = jnp.where(causal[None], score, -jnp.inf)
        _, idx = lax.top_k(score, topk)
        kvg = jax.vmap(lambda kk, ii: kk[ii])(kv, idx)
        kg, vg = jnp.split(kvg, 2, axis=-1)
        valid = idx <= qpos[None, :, None]
        att = jnp.einsum("bthd,btkd->bthk", qb, kg, preferred_element_type=jnp.float32) * (HEAD_DIM ** -0.5)
        att = jnp.where(valid[:, :, None, :], att, -jnp.inf)
        p = jax.nn.softmax(att, axis=-1)
        return jnp.einsum("bthk,btkd->bthd", p.astype(vg.dtype), vg)

    out = lax.map(block, jnp.arange(S // Q_BLOCK))
    return jnp.transpose(out, (1, 0, 2, 3, 4)).reshape(B, S, DSA_WIDTH)


def memory_attention(mq, mem, wk, wv):
    B, S = mq.shape[0], mq.shape[1]
    M = mem.shape[1]
    q = mq.reshape(B, S, MEM_HEADS, MEM_HEAD_DIM)
    k = (mem @ wk).reshape(B, M, MEM_HEADS, MEM_HEAD_DIM)
    v = (mem @ wv).reshape(B, M, MEM_HEADS, MEM_HEAD_DIM)
    s = jnp.einsum("bthd,bmhd->bhtm", q, k, preferred_element_type=jnp.float32) * (MEM_HEAD_DIM ** -0.5)
    p = jax.nn.softmax(s, axis=-1)
    o = jnp.einsum("bhtm,bmhd->bthd", p.astype(v.dtype), v)
    return o.reshape(B, S, MEM_WIDTH)


def spatial_gating(u, v, ln_g, ln_b, ws, bs):
    B, S = u.shape[0], u.shape[1]
    u = jax.nn.gelu(u)
    v = layer_norm(jax.nn.gelu(v), ln_g, ln_b)
    vc = v.reshape(B, S // SG_CHUNK, SG_CHUNK, SG_GROUPS, SG_WIDTH // SG_GROUPS)
    tril = jnp.tril(jnp.ones((SG_CHUNK, SG_CHUNK), dtype=bool))
    w = jnp.where(tril[None], ws, jnp.zeros_like(ws))
    mixed = jnp.einsum("gts,bcsgd->bctgd", w, vc) + jnp.transpose(bs)[None, None, :, :, None]
    return u * mixed.reshape(B, S, SG_WIDTH)


def even_layer(x, mem, cos, sin, w_in, conv_w, conv_b, cln_g, cln_b, pw2_w, pw2_b, w_out, mem_wk, mem_wv):
    B, S = x.shape[0], x.shape[1]
    (a_val, a_glu, a_gate, q, k, v, qi, ki, wi, b_gate, mq, m_gate) = _split(x @ w_in, EVEN_SPLITS)
    ya = conformer_conv(a_val, a_glu, conv_w, conv_b, cln_g, cln_b, pw2_w, pw2_b) * jax.nn.silu(a_gate)
    q = apply_rope(q.reshape(B, S, DSA_HEADS, HEAD_DIM), cos, sin)
    k = apply_rope(k[:, :, None, :], cos, sin)[:, :, 0]
    qi = apply_rope(qi.reshape(B, S, IDX_HEADS, IDX_DIM), cos, sin)
    ki = apply_rope(ki[:, :, None, :], cos, sin)[:, :, 0]
    wi = wi * ((IDX_HEADS ** -0.5) * (IDX_DIM ** -0.5))
    yb = dsa_attention(q, jnp.concatenate([k, v], axis=-1), qi, ki, wi) * jax.nn.silu(b_gate)
    ym = memory_attention(mq, mem, mem_wk, mem_wv) * jax.nn.silu(m_gate)
    return jnp.concatenate([ya, yb, ym], axis=-1) @ w_out


def odd_layer(x, mem, vln_g, vln_b, ws, bs, w_in, w_out, mem_wk, mem_wv):
    (u, v, c_gate, mq, m_gate) = _split(x @ w_in, ODD_SPLITS)
    yc = spatial_gating(u, v, vln_g, vln_b, ws, bs) * jax.nn.silu(c_gate)
    ym = memory_attention(mq, mem, mem_wk, mem_wv) * jax.nn.silu(m_gate)
    return jnp.concatenate([yc, ym], axis=-1) @ w_out


def setup_inputs(seed: int = 0) -> dict:
    key = jax.random.key(seed)
    ks = jax.random.split(key, 24)
    nrm = lambda k, shape, scale: jax.random.normal(k, shape, jnp.float32) * scale
    D = D_MODEL
    x = nrm(ks[0], (BATCH, SEQ, D), 1.0)
    mem = nrm(ks[1], (BATCH, MEM_LEN, D), 1.0)
    offset = jax.random.randint(ks[2], (BATCH, 1), 0, 1024, dtype=jnp.int32)
    positions = (offset + jnp.arange(SEQ, dtype=jnp.int32)[None, :]).astype(jnp.int32)
    return {
        "x": x,
        "mem": mem,
        "positions": positions,
        "e_w_in": nrm(ks[3], (N_EVEN, D, EVEN_IN), D ** -0.5),
        "e_conv_w": nrm(ks[4], (N_EVEN, CONV_K, CONV_WIDTH), CONV_K ** -0.5),
        "e_conv_b": nrm(ks[5], (N_EVEN, CONV_WIDTH), 0.02),
        "e_cln_g": 1.0 + nrm(ks[6], (N_EVEN, CONV_WIDTH), 0.02),
        "e_cln_b": nrm(ks[7], (N_EVEN, CONV_WIDTH), 0.02),
        "e_pw2_w": nrm(ks[8], (N_EVEN, CONV_WIDTH, CONV_WIDTH), CONV_WIDTH ** -0.5),
        "e_pw2_b": nrm(ks[9], (N_EVEN, CONV_WIDTH), 0.02),
        "e_w_out": nrm(ks[10], (N_EVEN, EVEN_OUT, D), DN_BETA * EVEN_OUT ** -0.5),
        "o_w_in": nrm(ks[11], (N_ODD, D, ODD_IN), D ** -0.5),
        "o_vln_g": 1.0 + nrm(ks[12], (N_ODD, SG_WIDTH), 0.02),
        "o_vln_b": nrm(ks[13], (N_ODD, SG_WIDTH), 0.02),
        "o_ws": nrm(ks[14], (N_ODD, SG_GROUPS, SG_CHUNK, SG_CHUNK), SG_CHUNK ** -0.5),
        "o_bs": 1.0 + nrm(ks[15], (N_ODD, SG_GROUPS, SG_CHUNK), 0.02),
        "o_w_out": nrm(ks[16], (N_ODD, ODD_OUT, D), DN_BETA * ODD_OUT ** -0.5),
        "mem_wk": nrm(ks[17], (DEPTH, D, MEM_WIDTH), D ** -0.5),
        "mem_wv": nrm(ks[18], (DEPTH, D, MEM_WIDTH), D ** -0.5),
        "ln_g": 1.0 + nrm(ks[19], (DEPTH, D), 0.02),
        "ln_b": nrm(ks[20], (DEPTH, D), 0.02),
    }


def reference(x, mem, positions, e_w_in, e_conv_w, e_conv_b, e_cln_g, e_cln_b, e_pw2_w, e_pw2_b, e_w_out,
              o_w_in, o_vln_g, o_vln_b, o_ws, o_bs, o_w_out, mem_wk, mem_wv, ln_g, ln_b):
    cos, sin = rope_tables(positions, HEAD_DIM)
    for layer in range(DEPTH):
        j = layer // 2
        if layer % 2 == 0:
            y = even_layer(x, mem, cos, sin, e_w_in[j], e_conv_w[j], e_conv_b[j], e_cln_g[j], e_cln_b[j],
                           e_pw2_w[j], e_pw2_b[j], e_w_out[j], mem_wk[layer], mem_wv[layer])
        else:
            y = odd_layer(x, mem, o_vln_g[j], o_vln_b[j], o_ws[j], o_bs[j], o_w_in[j], o_w_out[j],
                          mem_wk[layer], mem_wv[layer])
        x = layer_norm(DN_ALPHA * x + y, ln_g[layer], ln_b[layer])
    return x
```

```python
import functools

import numpy as np
import jax
import jax.numpy as jnp
from jax import lax
from jax.experimental import pallas as pl
from jax.experimental.pallas import tpu as pltpu

F32 = jnp.float32
BF16 = jnp.bfloat16

D_MODEL = 1024
DEPTH = 4
HEAD_DIM = 64
CONV_WIDTH = 512
CONV_K = 31
DSA_HEADS = 8
DSA_WIDTH = 512
IDX_HEADS = 4
IDX_DIM = 64
TOPK_MAX = 256
SG_CHUNK = 128
SG_GROUPS = 8
SG_WIDTH = 1024
MEM_HEADS = 4
MEM_HEAD_DIM = 128
MEM_WIDTH = 512
ROPE_THETA = 10000.0
LN_EPS = 1e-5
DN_ALPHA = (2 * DEPTH) ** 0.25

LANES = 128
CONV_HALO = 32
NEG = -1e30
INT_MIN = -(2 ** 31)

VMEM_LIMIT = 48 * 1024 * 1024

_NT = (((1,), (1,)), ((), ()))


def _cparams(n_axes):
    return pltpu.CompilerParams(dimension_semantics=("arbitrary",) * n_axes,
                                vmem_limit_bytes=VMEM_LIMIT)


def _silu(g):
    return g * (1.0 / (1.0 + jnp.exp(-g)))


def _gelu_tanh(x):
    c = np.float32(np.sqrt(2.0 / np.pi))
    return x * (0.5 * (1.0 + jnp.tanh(c * (x + 0.044715 * (x * x * x)))))


def _layer_norm(x, g, b):
    mu = jnp.mean(x, axis=-1, keepdims=True)
    xc = x - mu
    var = jnp.mean(xc * xc, axis=-1, keepdims=True)
    return xc * lax.rsqrt(var + LN_EPS) * g + b


def _mm_kernel(x_ref, w_ref, o_ref):
    o_ref[...] = jnp.dot(x_ref[...].astype(BF16), w_ref[...],
                         preferred_element_type=F32).astype(o_ref.dtype)


def _matmul(x, w, *, tm, tn, out_dtype=F32):
    m, k = x.shape
    n = w.shape[1]
    return pl.pallas_call(
        _mm_kernel,
        grid=(m // tm, n // tn),
        in_specs=[pl.BlockSpec((tm, k), lambda i, j: (i, 0)),
                  pl.BlockSpec((k, tn), lambda i, j: (0, j))],
        out_specs=pl.BlockSpec((tm, tn), lambda i, j: (i, j)),
        out_shape=jax.ShapeDtypeStruct((m, n), out_dtype),
        compiler_params=_cparams(2),
    )(x, w)


def _rope_table_kernel(pos_ref, inv_ref, cos_ref, sin_ref):
    ang = pos_ref[...].astype(F32) * inv_ref[...]
    cos_ref[...] = jnp.cos(ang)
    sin_ref[...] = jnp.sin(ang)


def _rope_tables(positions):
    n = positions.size
    half = HEAD_DIM // 2
    per_row = LANES // half
    inv = (ROPE_THETA ** (-jnp.arange(0, HEAD_DIM, 2, dtype=F32) / HEAD_DIM)).astype(F32)
    pos_rep = jnp.repeat(positions.reshape(n // per_row, per_row), half, axis=1)
    inv_rep = jnp.tile(inv, per_row).reshape(1, LANES)
    rows = n // per_row
    tr = min(512, rows)
    cos, sin = pl.pallas_call(
        _rope_table_kernel,
        grid=(rows // tr,),
        in_specs=[pl.BlockSpec((tr, LANES), lambda i: (i, 0)),
                  pl.BlockSpec((1, LANES), lambda i: (0, 0))],
        out_specs=[pl.BlockSpec((tr, LANES), lambda i: (i, 0))] * 2,
        out_shape=[jax.ShapeDtypeStruct((rows, LANES), F32)] * 2,
        compiler_params=_cparams(1),
    )(pos_rep, inv_rep)
    cos = cos.reshape(n, half)
    sin = sin.reshape(n, half)
    z = jnp.zeros((n, 2 * half), F32)
    return jnp.concatenate([cos, cos, z], axis=1), jnp.concatenate([-sin, sin, z], axis=1)


def _prep_kernel(q_ref, qi_ref, kvw_ref, cos_ref, sin_ref,
                 q8_ref, qio_ref, wi_ref, ka_ref, ve_ref, ki_ref):
    cosf = cos_ref[...]
    sinf = sin_ref[...]
    lane = lax.broadcasted_iota(jnp.int32, cosf.shape, 1)
    first = lane < HEAD_DIM // 2

    def rope(x):
        rot = jnp.where(first, pltpu.roll(x, LANES - HEAD_DIM // 2, 1), pltpu.roll(x, HEAD_DIM // 2, 1))
        return x * cosf + rot * sinf

    for g in range(DSA_HEADS):
        sl = slice(LANES * g, LANES * (g + 1))
        q8_ref[:, sl] = (rope(q_ref[:, sl]) * (HEAD_DIM ** -0.5)).astype(BF16)
    for g in range(IDX_HEADS):
        sl = slice(LANES * g, LANES * (g + 1))
        qio_ref[:, sl] = rope(qi_ref[:, sl]).astype(BF16)
    ka_ref[...] = rope(kvw_ref[:, 0:LANES]).astype(BF16)
    ve_ref[...] = jnp.where(lane == HEAD_DIM, 1.0, kvw_ref[:, LANES:2 * LANES]).astype(BF16)
    ki_ref[...] = rope(kvw_ref[:, 2 * LANES:3 * LANES]).astype(BF16)
    wi_ref[...] = kvw_ref[:, 3 * LANES:4 * LANES] * ((IDX_HEADS ** -0.5) * (IDX_DIM ** -0.5))


def _prep(h, cosf, sinf, col_q, col_qi, col_kvw, *, tt):
    n = h.shape[0]
    wq = DSA_HEADS * LANES
    wqi = IDX_HEADS * LANES
    wkv = 4 * LANES
    row = lambda i: (i, 0)
    return pl.pallas_call(
        _prep_kernel,
        grid=(n // tt,),
        in_specs=[pl.BlockSpec((tt, wq), lambda i: (i, col_q // wq)),
                  pl.BlockSpec((tt, wqi), lambda i: (i, col_qi // wqi)),
                  pl.BlockSpec((tt, wkv), lambda i: (i, col_kvw // wkv)),
                  pl.BlockSpec((tt, LANES), row),
                  pl.BlockSpec((tt, LANES), row)],
        out_specs=[pl.BlockSpec((tt, wq), row), pl.BlockSpec((tt, wqi), row), pl.BlockSpec((tt, LANES), row),
                   pl.BlockSpec((tt, LANES), row), pl.BlockSpec((tt, LANES), row), pl.BlockSpec((tt, LANES), row)],
        out_shape=[jax.ShapeDtypeStruct((n, wq), BF16), jax.ShapeDtypeStruct((n, wqi), BF16),
                   jax.ShapeDtypeStruct((n, LANES), F32), jax.ShapeDtypeStruct((n, LANES), BF16),
                   jax.ShapeDtypeStruct((n, LANES), BF16), jax.ShapeDtypeStruct((n, LANES), BF16)],
        compiler_params=_cparams(1),
    )(h, h, h, cosf, sinf)


def _dsa_kernel(q8_ref, qi_ref, wi_ref, bg_ref, ka_ref, ve_ref, ki_ref, o_ref,
                keys_ref, bias_ref, p_ref, acc_ref, m_ref, *, tq, tk, topk):
    i = pl.program_id(1)
    nkb = (i * tq + tq + tk - 1) // tk
    nl = tk // LANES
    t_idx = i * tq + lax.broadcasted_iota(jnp.int32, (tq, tk), 0)
    s_loc = lax.broadcasted_iota(jnp.int32, (tq, tk), 1)

    qis = jnp.concatenate([qi_ref[:, LANES * h:LANES * (h + 1)] for h in range(IDX_HEADS)], axis=0)
    wi = wi_ref[...]

    def score_block(j, carry):
        kib = ki_ref[pl.ds(pl.multiple_of(j * tk, tk), tk), :]
        lg = lax.dot_general(qis, kib, _NT, preferred_element_type=F32)
        sc = jnp.maximum(lg[0:tq], 0.0) * wi[:, 0:1]
        for h in range(1, IDX_HEADS):
            sc = sc + jnp.maximum(lg[h * tq:(h + 1) * tq], 0.0) * wi[:, h:h + 1]
        bits = pltpu.bitcast(sc, jnp.int32)
        key = bits ^ ((bits >> 31) & 0x7FFFFFFF)
        keys_ref[j] = jnp.where(j * tk + s_loc <= t_idx, key, INT_MIN)
        return carry

    lax.fori_loop(0, nkb, score_block, 0)

    def count(pred_fn):
        def blk(j, acc):
            kb = keys_ref[j]
            for c in range(nl):
                acc = acc + jnp.where(pred_fn(kb[:, LANES * c:LANES * (c + 1)]), 1.0, 0.0)
            return acc
        acc = lax.fori_loop(0, nkb, blk, jnp.zeros((tq, LANES), F32))
        return jnp.sum(acc, axis=-1, keepdims=True)

    def bit_step(b, prefix):
        cand = prefix | lax.shift_left(jnp.int32(1), 31 - b)
        cand_s = cand ^ INT_MIN
        cnt = count(lambda kb: kb >= cand_s)
        return jnp.where(cnt >= topk, cand, prefix)

    prefix = lax.fori_loop(0, 32, bit_step, jnp.zeros((tq, LANES), jnp.int32))
    thr = prefix ^ INT_MIN
    c_gt = count(lambda kb: kb > thr)
    n_tie = jnp.where(thr[:, 0:1] == INT_MIN, 0.0, topk - c_gt)
    thr_full = jnp.concatenate([thr] * nl, axis=1)
    upper = jnp.where(lax.broadcasted_iota(jnp.int32, (tk, tk), 0) < lax.broadcasted_iota(jnp.int32, (tk, tk), 1),
                      1.0, 0.0).astype(BF16)

    def select_block(j, off):
        kb = keys_ref[j]
        tie = kb == thr_full
        tie_f = jnp.where(tie, 1.0, 0.0)
        before = jnp.dot(tie_f.astype(BF16), upper, preferred_element_type=F32) + off
        bias_ref[j] = jnp.where(kb > thr_full, 0.0,
                                jnp.where(tie, jnp.where(before < n_tie, 0.0, NEG), NEG))
        return off + jnp.sum(tie_f, axis=-1, keepdims=True)

    lax.fori_loop(0, nkb, select_block, jnp.zeros((tq, 1), F32))

    qs = jnp.concatenate([q8_ref[:, LANES * h:LANES * (h + 1)] for h in range(DSA_HEADS)], axis=0)
    m_ref[...] = jnp.full(m_ref.shape, NEG, F32)
    acc_ref[...] = jnp.zeros(acc_ref.shape, F32)

    def attend_block(j, carry):
        rows = pl.ds(pl.multiple_of(j * tk, tk), tk)
        s = lax.dot_general(qs, ka_ref[rows, :], _NT, preferred_element_type=F32)
        bias = bias_ref[j]
        for h in range(DSA_HEADS):
            hs = slice(h * tq, (h + 1) * tq)
            sh = s[hs] + bias
            m_old = m_ref[hs, :]
            m_new = jnp.maximum(m_old, jnp.max(sh, axis=-1, keepdims=True))
            p_ref[hs, :] = jnp.exp(sh - jnp.concatenate([m_new] * nl, axis=1)).astype(BF16)
            acc_ref[hs, :] = acc_ref[hs, :] * jnp.exp(m_old - m_new)
            m_ref[hs, :] = m_new
        acc_ref[...] += jnp.dot(p_ref[...], ve_ref[rows, :], preferred_element_type=F32)
        return carry

    lax.fori_loop(0, nkb, attend_block, 0)

    lane = lax.broadcasted_iota(jnp.int32, (tq, LANES), 1)

    def head_out(h):
        a = acc_ref[h * tq:(h + 1) * tq, :]
        denom = jnp.sum(jnp.where(lane == HEAD_DIM, a, 0.0), axis=-1, keepdims=True)
        return a / denom

    pairs = [jnp.where(lane < HEAD_DIM, head_out(2 * g), pltpu.roll(head_out(2 * g + 1), HEAD_DIM, 1))
             for g in range(DSA_HEADS // 2)]
    o_ref[...] = jnp.concatenate(pairs, axis=1) * _silu(bg_ref[...])


def _dsa(q8, qi, wi, h, ka, ve, ki, col_bgate, *, batch, seq, tq, tk):
    n = q8.shape[0]
    nq = seq // tq
    topk = min(TOPK_MAX, seq // 4)
    qrow = lambda b, i: (b * nq + i, 0)
    krow = lambda b, i: (b, 0)
    kernel = functools.partial(_dsa_kernel, tq=tq, tk=tk, topk=float(topk))
    return pl.pallas_call(
        kernel,
        grid=(batch, nq),
        in_specs=[pl.BlockSpec((tq, DSA_HEADS * LANES), qrow),
                  pl.BlockSpec((tq, IDX_HEADS * LANES), qrow),
                  pl.BlockSpec((tq, LANES), qrow),
                  pl.BlockSpec((tq, DSA_WIDTH), lambda b, i: (b * nq + i, col_bgate // DSA_WIDTH)),
                  pl.BlockSpec((seq, LANES), krow),
                  pl.BlockSpec((seq, LANES), krow),
                  pl.BlockSpec((seq, LANES), krow)],
        out_specs=pl.BlockSpec((tq, DSA_WIDTH), qrow),
        out_shape=jax.ShapeDtypeStruct((n, DSA_WIDTH), F32),
        scratch_shapes=[pltpu.VMEM((seq // tk, tq, tk), jnp.int32),
                        pltpu.VMEM((seq // tk, tq, tk), F32),
                        pltpu.VMEM((DSA_HEADS * tq, tk), BF16),
                        pltpu.VMEM((DSA_HEADS * tq, LANES), F32),
                        pltpu.VMEM((DSA_HEADS * tq, LANES), F32)],
        compiler_params=_cparams(2),
    )(q8, qi, wi, h, ka, ve, ki)


def _conv_kernel(val_ref, glu_ref, gate_ref, cw_ref, cb_ref, g_ref, b_ref, pw_ref, pb_ref, o_ref, hist_ref, *, tt):
    @pl.when(pl.program_id(1) == 0)
    def _():
        hist_ref[0:CONV_HALO, :] = jnp.zeros((CONV_HALO, CONV_WIDTH), F32)

    glu = glu_ref[...]
    hist_ref[CONV_HALO:CONV_HALO + tt, :] = val_ref[...] * (1.0 / (1.0 + jnp.exp(-glu)))
    base = CONV_HALO - (CONV_K - 1)
    acc = hist_ref[pl.ds(base, tt), :] * cw_ref[0:1, :]
    for j in range(1, CONV_K):
        acc = acc + hist_ref[pl.ds(base + j, tt), :] * cw_ref[j:j + 1, :]
    acc = acc + cb_ref[...]
    hist_ref[0:CONV_HALO, :] = hist_ref[tt:tt + CONV_HALO, :]
    y = _silu(_layer_norm(acc, g_ref[...], b_ref[...]))
    y = jnp.dot(y.astype(BF16), pw_ref[...], preferred_element_type=F32) + pb_ref[...]
    o_ref[...] = y * _silu(gate_ref[...])


def _conv_branch(h, conv_w, conv_b, ln_g, ln_b, pw_w, pw_b, *, batch, seq, tt):
    n = h.shape[0]
    nt = seq // tt
    cw = jnp.concatenate([conv_w, jnp.zeros((CONV_HALO - CONV_K, CONV_WIDTH), F32)], axis=0)
    vec = lambda v: v.reshape(1, CONV_WIDTH)
    const = lambda b, i: (0, 0)
    return pl.pallas_call(
        functools.partial(_conv_kernel, tt=tt),
        grid=(batch, nt),
        in_specs=[pl.BlockSpec((tt, CONV_WIDTH), lambda b, i: (b * nt + i, 0)),
                  pl.BlockSpec((tt, CONV_WIDTH), lambda b, i: (b * nt + i, 1)),
                  pl.BlockSpec((tt, CONV_WIDTH), lambda b, i: (b * nt + i, 2)),
                  pl.BlockSpec((CONV_HALO, CONV_WIDTH), const),
                  pl.BlockSpec((1, CONV_WIDTH), const),
                  pl.BlockSpec((1, CONV_WIDTH), const),
                  pl.BlockSpec((1, CONV_WIDTH), const),
                  pl.BlockSpec((CONV_WIDTH, CONV_WIDTH), const),
                  pl.BlockSpec((1, CONV_WIDTH), const)],
        out_specs=pl.BlockSpec((tt, CONV_WIDTH), lambda b, i: (b * nt + i, 0)),
        out_shape=jax.ShapeDtypeStruct((n, CONV_WIDTH), F32),
        scratch_shapes=[pltpu.VMEM((CONV_HALO + tt, CONV_WIDTH), F32)],
        compiler_params=_cparams(2),
    )(h, h, h, cw, vec(conv_b), vec(ln_g), vec(ln_b), pw_w.astype(BF16), vec(pw_b))


def _mem_kernel(mq_ref, mg_ref, kv_ref, o_ref):
    outs = []
    for h in range(MEM_HEADS):
        hs = slice(MEM_HEAD_DIM * h, MEM_HEAD_DIM * (h + 1))
        vs = slice(MEM_WIDTH + MEM_HEAD_DIM * h, MEM_WIDTH + MEM_HEAD_DIM * (h + 1))
        s = lax.dot_general(mq_ref[:, hs].astype(BF16), kv_ref[:, hs], _NT,
                            preferred_element_type=F32) * (MEM_HEAD_DIM ** -0.5)
        e = jnp.exp(s - jnp.max(s, axis=-1, keepdims=True))
        p = e / jnp.sum(e, axis=-1, keepdims=True)
        outs.append(jnp.dot(p.astype(BF16), kv_ref[:, vs], preferred_element_type=F32))
    o_ref[...] = jnp.concatenate(outs, axis=1) * _silu(mg_ref[...])


def _mem_attention(h, mem_kv, col_mq, col_gate, *, seq, mem_len, tt):
    n = h.shape[0]
    nt = seq // tt
    return pl.pallas_call(
        _mem_kernel,
        grid=(n // tt,),
        in_specs=[pl.BlockSpec((tt, MEM_WIDTH), lambda i: (i, col_mq // MEM_WIDTH)),
                  pl.BlockSpec((tt, MEM_WIDTH), lambda i: (i, col_gate // MEM_WIDTH)),
                  pl.BlockSpec((mem_len, 2 * MEM_WIDTH), lambda i: (i // nt, 0))],
        out_specs=pl.BlockSpec((tt, MEM_WIDTH), lambda i: (i, 0)),
        out_shape=jax.ShapeDtypeStruct((n, MEM_WIDTH), F32),
        compiler_params=_cparams(1),
    )(h, h, mem_kv)


def _sgu_kernel(u_ref, v_ref, gate_ref, g_ref, b_ref, ws_ref, bs_ref, o_ref, vn_ref, *, tt):
    vn_ref[...] = _layer_norm(_gelu_tanh(v_ref[...]), g_ref[...], b_ref[...]).astype(BF16)
    causal = (lax.broadcasted_iota(jnp.int32, (SG_CHUNK, SG_CHUNK), 1)
              <= lax.broadcasted_iota(jnp.int32, (SG_CHUNK, SG_CHUNK), 0))
    gw = SG_WIDTH // SG_GROUPS
    for g in range(SG_GROUPS):
        w = jnp.where(causal, ws_ref[g], 0.0).astype(BF16)
        bias = bs_ref[:, g:g + 1]
        cs = slice(gw * g, gw * (g + 1))
        for c in range(tt // SG_CHUNK):
            rs = slice(SG_CHUNK * c, SG_CHUNK * (c + 1))
            mixed = jnp.dot(w, vn_ref[rs, cs], preferred_element_type=F32) + bias
            o_ref[rs, cs] = _gelu_tanh(u_ref[rs, cs]) * mixed * _silu(gate_ref[rs, cs])


def _spatial_gating(h, ln_g, ln_b, ws, bs, *, tt):
    n = h.shape[0]
    const2 = lambda i: (0, 0)
    return pl.pallas_call(
        functools.partial(_sgu_kernel, tt=tt),
        grid=(n // tt,),
        in_specs=[pl.BlockSpec((tt, SG_WIDTH), lambda i: (i, 0)),
                  pl.BlockSpec((tt, SG_WIDTH), lambda i: (i, 1)),
                  pl.BlockSpec((tt, SG_WIDTH), lambda i: (i, 2)),
                  pl.BlockSpec((1, SG_WIDTH), const2),
                  pl.BlockSpec((1, SG_WIDTH), const2),
                  pl.BlockSpec((SG_GROUPS, SG_CHUNK, SG_CHUNK), lambda i: (0, 0, 0)),
                  pl.BlockSpec((SG_CHUNK, SG_GROUPS), const2)],
        out_specs=pl.BlockSpec((tt, SG_WIDTH), lambda i: (i, 0)),
        out_shape=jax.ShapeDtypeStruct((n, SG_WIDTH), F32),
        scratch_shapes=[pltpu.VMEM((tt, SG_WIDTH), BF16)],
        compiler_params=_cparams(1),
    )(h, h, h, ln_g.reshape(1, SG_WIDTH), ln_b.reshape(1, SG_WIDTH), ws, jnp.transpose(bs))


def _out_kernel(*refs, widths):
    ys = refs[:len(widths)]
    w_ref, x_ref, g_ref, b_ref, o_ref = refs[len(widths):]
    z = DN_ALPHA * x_ref[...]
    row = 0
    for y_ref, wd in zip(ys, widths):
        z = z + jnp.dot(y_ref[...].astype(BF16), w_ref[row:row + wd, :], preferred_element_type=F32)
        row += wd
    o_ref[...] = _layer_norm(z, g_ref[...], b_ref[...])


def _out_norm(ys, w_out, x, ln_g, ln_b, *, tt):
    n = x.shape[0]
    widths = tuple(y.shape[1] for y in ys)
    row = lambda i: (i, 0)
    const = lambda i: (0, 0)
    return pl.pallas_call(
        functools.partial(_out_kernel, widths=widths),
        grid=(n // tt,),
        in_specs=[pl.BlockSpec((tt, wd), row) for wd in widths]
        + [pl.BlockSpec(w_out.shape, const), pl.BlockSpec((tt, D_MODEL), row),
           pl.BlockSpec((1, D_MODEL), const), pl.BlockSpec((1, D_MODEL), const)],
        out_specs=pl.BlockSpec((tt, D_MODEL), row),
        out_shape=jax.ShapeDtypeStruct((n, D_MODEL), F32),
        compiler_params=_cparams(1),
    )(*ys, w_out.astype(BF16), x, ln_g.reshape(1, D_MODEL), ln_b.reshape(1, D_MODEL))


def _pad_heads(w, heads, dim):
    d = w.shape[0]
    w = w.reshape(d, heads, dim)
    return jnp.pad(w, ((0, 0), (0, 0), (0, LANES - dim))).reshape(d, heads * LANES)


_E_GATE_B = 3 * CONV_WIDTH
_E_MQ = _E_GATE_B + DSA_WIDTH
_E_MGATE = _E_MQ + MEM_WIDTH
_E_Q = _E_MGATE + MEM_WIDTH
_E_QI = _E_Q + DSA_HEADS * LANES
_E_KVW = _E_QI + IDX_HEADS * LANES
_E_TOTAL = _E_KVW + 4 * LANES


def _even_w_in(w):
    sizes = (CONV_WIDTH, CONV_WIDTH, CONV_WIDTH, DSA_WIDTH, HEAD_DIM, HEAD_DIM,
             IDX_HEADS * IDX_DIM, IDX_DIM, IDX_HEADS, DSA_WIDTH, MEM_WIDTH, MEM_WIDTH)
    (a_val, a_glu, a_gate, q, k, v, qi, ki, wi, b_gate, mq, m_gate) = jnp.split(
        w, tuple(int(c) for c in np.cumsum(sizes)[:-1]), axis=1)
    cols = [a_val, a_glu, a_gate, b_gate, mq, m_gate,
            _pad_heads(q, DSA_HEADS, HEAD_DIM), _pad_heads(qi, IDX_HEADS, IDX_DIM),
            _pad_heads(k, 1, HEAD_DIM), _pad_heads(v, 1, HEAD_DIM), _pad_heads(ki, 1, IDX_DIM),
            _pad_heads(wi, 1, IDX_HEADS)]
    return jnp.concatenate(cols, axis=1).astype(BF16)


def _even_layer(x, mem_kv, cosf, sinf, w_in, conv_w, conv_b, cln_g, cln_b, pw2_w, pw2_b, w_out, ln_g, ln_b,
                *, batch, seq, mem_len, tt):
    h = _matmul(x, _even_w_in(w_in), tm=tt, tn=1024)
    ya = _conv_branch(h, conv_w, conv_b, cln_g, cln_b, pw2_w, pw2_b, batch=batch, seq=seq, tt=tt)
    q8, qi, wi, ka, ve, ki = _prep(h, cosf, sinf, _E_Q, _E_QI, _E_KVW, tt=tt)
    yb = _dsa(q8, qi, wi, h, ka, ve, ki, _E_GATE_B, batch=batch, seq=seq, tq=128, tk=256)
    ym = _mem_attention(h, mem_kv, _E_MQ, _E_MGATE, seq=seq, mem_len=mem_len, tt=tt)
    return _out_norm([ya, yb, ym], w_out, x, ln_g, ln_b, tt=tt)


def _odd_layer(x, mem_kv, vln_g, vln_b, ws, bs, w_in, w_out, ln_g, ln_b, *, seq, mem_len, tt):
    h = _matmul(x, w_in.astype(BF16), tm=tt, tn=1024)
    yc = _spatial_gating(h, vln_g, vln_b, ws, bs, tt=tt)
    ym = _mem_attention(h, mem_kv, 3 * SG_WIDTH, 3 * SG_WIDTH + MEM_WIDTH, seq=seq, mem_len=mem_len, tt=tt)
    return _out_norm([yc, ym], w_out, x, ln_g, ln_b, tt=tt)


def kernel(x, mem, positions, e_w_in, e_conv_w, e_conv_b, e_cln_g, e_cln_b, e_pw2_w, e_pw2_b, e_w_out, o_w_in, o_vln_g, o_vln_b, o_ws, o_bs, o_w_out, mem_wk, mem_wv, ln_g, ln_b):
    batch, seq, d = x.shape
    mem_len = mem.shape[1]
    tt = 512
    cosf, sinf = _rope_tables(positions)
    xf = x.reshape(batch * seq, d)
    memf = mem.reshape(batch * mem_len, d)
    for layer in range(DEPTH):
        j = layer // 2
        w_kv = jnp.concatenate([mem_wk[layer], mem_wv[layer]], axis=1).astype(BF16)
        mem_kv = _matmul(memf, w_kv, tm=min(tt, memf.shape[0]), tn=1024, out_dtype=BF16)
        if layer % 2 == 0:
            xf = _even_layer(xf, mem_kv, cosf, sinf, e_w_in[j], e_conv_w[j], e_conv_b[j], e_cln_g[j], e_cln_b[j],
                             e_pw2_w[j], e_pw2_b[j], e_w_out[j], ln_g[layer], ln_b[layer],
                             batch=batch, seq=seq, mem_len=mem_len, tt=tt)
        else:
            xf = _odd_layer(xf, mem_kv, o_vln_g[j], o_vln_b[j], o_ws[j], o_bs[j], o_w_in[j], o_w_out[j],
                            ln_g[layer], ln_b[layer], seq=seq, mem_len=mem_len, tt=tt)
    return xf.reshape(batch, seq, d)
```

```python
import functools

import numpy as np
import jax
import jax.numpy as jnp
from jax import lax
from jax.experimental import pallas as pl
from jax.experimental.pallas import tpu as pltpu

F32 = jnp.float32
BF16 = jnp.bfloat16

D_MODEL = 1024
DEPTH = 4
HEAD_DIM = 64
CONV_WIDTH = 512
CONV_K = 31
DSA_HEADS = 8
DSA_WIDTH = 512
IDX_HEADS = 4
IDX_DIM = 64
TOPK_MAX = 256
SG_CHUNK = 128
SG_GROUPS = 8
SG_WIDTH = 1024
MEM_HEADS = 4
MEM_HEAD_DIM = 128
MEM_WIDTH = 512
ROPE_THETA = 10000.0
LN_EPS = 1e-5
DN_ALPHA = (2 * DEPTH) ** 0.25

LANES = 128
SUBLANES = 8
VE_ROWS = HEAD_DIM + 16
CONV_HALO = 32
NEG = -1e30
INT_MIN = -(2 ** 31)
HALF16 = 2 ** 15
QK_SCALE_LOG2 = float(np.log2(np.e)) * HEAD_DIM ** -0.5
BOUND_MARGIN = 1.0 + 2.0 ** -8
Q_NORM_GUESS = QK_SCALE_LOG2 * HEAD_DIM ** 0.5
TINY_DENOM = 2.0 ** -100

VMEM_LIMIT = 56 * 1024 * 1024

TOKEN_TILE = 512
Q_TILE = 256
K_TILE = 256

_NT = (((1,), (1,)), ((), ()))


def _cparams(n_axes, flags=None):
    return pltpu.CompilerParams(dimension_semantics=("arbitrary",) * n_axes,
                                vmem_limit_bytes=VMEM_LIMIT, flags=flags)


def _resident(shape):
    zeros = (0,) * len(shape)
    return pl.BlockSpec(shape, lambda *_: zeros, pipeline_mode=pl.Buffered(1))


def _silu(g):
    return g * (1.0 / (1.0 + jnp.exp(-g)))


def _gelu_tanh(x):
    c = np.float32(np.sqrt(2.0 / np.pi))
    return x * (0.5 * (1.0 + jnp.tanh(c * (x + 0.044715 * (x * x * x)))))


def _layer_norm(x, g, b):
    mu = jnp.mean(x, axis=-1, keepdims=True)
    xc = x - mu
    var = jnp.mean(xc * xc, axis=-1, keepdims=True)
    return xc * lax.rsqrt(var + LN_EPS) * g + b


def _mem_heads(mq, kv_ref):
    outs = []
    for h in range(MEM_HEADS):
        hs = slice(MEM_HEAD_DIM * h, MEM_HEAD_DIM * (h + 1))
        vs = slice(MEM_WIDTH + MEM_HEAD_DIM * h, MEM_WIDTH + MEM_HEAD_DIM * (h + 1))
        s = lax.dot_general(mq[:, hs].astype(BF16), kv_ref[:, hs], _NT,
                            preferred_element_type=F32) * (MEM_HEAD_DIM ** -0.5)
        e = jnp.exp(s - jnp.max(s, axis=-1, keepdims=True))
        p = e / jnp.sum(e, axis=-1, keepdims=True)
        outs.append(jnp.dot(p.astype(BF16), kv_ref[:, vs], preferred_element_type=F32))
    return jnp.concatenate(outs, axis=1)


def _mm_kernel(x_ref, w_ref, o_ref):
    o_ref[...] = jnp.dot(x_ref[...].astype(BF16), w_ref[...],
                         preferred_element_type=F32).astype(o_ref.dtype)


def _matmul(x, w, *, tm, tn, out_dtype=F32):
    m, k = x.shape
    n = w.shape[1]
    return pl.pallas_call(
        _mm_kernel,
        name="mem_kv_proj",
        grid=(m // tm, n // tn),
        in_specs=[pl.BlockSpec((tm, k), lambda i, j: (i, 0)),
                  pl.BlockSpec((k, tn), lambda i, j: (0, j))],
        out_specs=pl.BlockSpec((tm, tn), lambda i, j: (i, j)),
        out_shape=jax.ShapeDtypeStruct((m, n), out_dtype),
        compiler_params=_cparams(2),
    )(x, w)


def _rope_table_kernel(pos_ref, inv_ref, cos_ref, sin_ref):
    ang = pos_ref[...].astype(F32) * inv_ref[...]
    cos_ref[...] = jnp.cos(ang)
    sin_ref[...] = jnp.sin(ang)


def _rope_tables(positions):
    n = positions.size
    half = HEAD_DIM // 2
    per_row = LANES // half
    rows = n // per_row
    inv = (ROPE_THETA ** (-jnp.arange(0, HEAD_DIM, 2, dtype=F32) / HEAD_DIM)).astype(F32)
    pos_rep = jnp.repeat(jnp.transpose(positions.reshape(per_row, rows)), half, axis=1)
    inv_rep = jnp.tile(inv, per_row).reshape(1, LANES)
    tr = min(512, rows)
    cos, sin = pl.pallas_call(
        _rope_table_kernel,
        name="rope_tables",
        grid=(rows // tr,),
        in_specs=[pl.BlockSpec((tr, LANES), lambda i: (i, 0)),
                  pl.BlockSpec((1, LANES), lambda i: (0, 0))],
        out_specs=[pl.BlockSpec((tr, LANES), lambda i: (i, 0))] * 2,
        out_shape=[jax.ShapeDtypeStruct((rows, LANES), F32)] * 2,
        compiler_params=_cparams(1),
    )(pos_rep, inv_rep)
    unpack = lambda t: jnp.concatenate([t[:, half * q:half * (q + 1)] for q in range(per_row)], axis=0)
    cos, sin = unpack(cos), unpack(sin)
    return jnp.concatenate([cos, cos, cos, cos], axis=1), jnp.concatenate([-sin, sin, -sin, sin], axis=1)


_E_GATE_B = 3 * CONV_WIDTH
_E_MQ = _E_GATE_B + DSA_WIDTH
_E_MGATE = _E_MQ + MEM_WIDTH
_E_Q = _E_MGATE + MEM_WIDTH
_E_QI = _E_Q + DSA_WIDTH
_E_KKI = _E_QI + IDX_HEADS * IDX_DIM
_E_VW = _E_KKI + LANES
_E_TOTAL = _E_VW + LANES


def _even_w_in(w):
    sizes = (CONV_WIDTH, CONV_WIDTH, CONV_WIDTH, DSA_WIDTH, HEAD_DIM, HEAD_DIM,
             IDX_HEADS * IDX_DIM, IDX_DIM, IDX_HEADS, DSA_WIDTH, MEM_WIDTH, MEM_WIDTH)
    (a_val, a_glu, a_gate, q, k, v, qi, ki, wi, b_gate, mq, m_gate) = jnp.split(
        w, tuple(int(c) for c in np.cumsum(sizes)[:-1]), axis=1)
    pad = jnp.zeros((w.shape[0], LANES - HEAD_DIM - IDX_HEADS), w.dtype)
    cols = [a_val, a_glu, a_gate, b_gate, mq, m_gate, q, qi, k, ki, v, wi, pad]
    return jnp.concatenate(cols, axis=1).astype(BF16)


def _even_front_kernel(x_ref, win_ref, wout_ref, kv_ref, cw_ref, cb_ref, cg_ref, cbeta_ref, pw_ref, pb_ref,
                       cos_ref, sin_ref,
                       zp_ref, bg_ref, q8_ref, qio_ref, wi_ref, ka_ref, vet_ref, ki_ref,
                       hist_ref, shift_ref, *, tt, tk):
    @pl.when(pl.program_id(1) == 0)
    def _():
        hist_ref[0:CONV_HALO, :] = jnp.zeros((CONV_HALO, CONV_WIDTH), F32)

    x = x_ref[...]
    xb = x.astype(BF16)

    def proj(col, width):
        return jnp.dot(xb, win_ref[:, col:col + width], preferred_element_type=F32)

    glu = proj(CONV_WIDTH, CONV_WIDTH)
    hist_ref[CONV_HALO:CONV_HALO + tt, :] = proj(0, CONV_WIDTH) * (1.0 / (1.0 + jnp.exp(-glu)))
    span = tt + CONV_HALO - SUBLANES
    for r in range(1, SUBLANES):
        shift_ref[r - 1, 0:span, :] = hist_ref[pl.ds(r, span), :]
    base = CONV_HALO - (CONV_K - 1)
    acc = None
    for j in range(CONV_K):
        a, r = divmod(base + j, SUBLANES)
        rows = slice(SUBLANES * a, SUBLANES * a + tt)
        src = hist_ref[rows, :] if r == 0 else shift_ref[r - 1, rows, :]
        term = src * cw_ref[j:j + 1, :]
        acc = term if acc is None else acc + term
    acc = acc + cb_ref[...]
    hist_ref[0:CONV_HALO, :] = hist_ref[tt:tt + CONV_HALO, :]
    y = _silu(_layer_norm(acc, cg_ref[...], cbeta_ref[...]))
    y = jnp.dot(y.astype(BF16), pw_ref[...], preferred_element_type=F32) + pb_ref[...]
    ya = y * _silu(proj(2 * CONV_WIDTH, CONV_WIDTH))
    z = DN_ALPHA * x + jnp.dot(ya.astype(BF16), wout_ref[0:CONV_WIDTH, :], preferred_element_type=F32)

    ym = _mem_heads(proj(_E_MQ, MEM_WIDTH), kv_ref) * _silu(proj(_E_MGATE, MEM_WIDTH))
    m0 = CONV_WIDTH + DSA_WIDTH
    zp_ref[...] = z + jnp.dot(ym.astype(BF16), wout_ref[m0:m0 + MEM_WIDTH, :], preferred_element_type=F32)

    bg_ref[...] = proj(_E_GATE_B, DSA_WIDTH)
    cosf = cos_ref[...]
    sinf = sin_ref[...]

    lane = lax.broadcasted_iota(jnp.int32, (tt, LANES), 1)
    first_half = (lane & (HEAD_DIM - 1)) < HEAD_DIM // 2
    low = lane < HEAD_DIM
    marker = jnp.where(lane == HEAD_DIM, 1.0, 0.0)

    def rope(g):
        rot = jnp.where(first_half, pltpu.roll(g, LANES - HEAD_DIM // 2, 1), pltpu.roll(g, HEAD_DIM // 2, 1))
        return g * cosf + rot * sinf

    qf = proj(_E_Q, DSA_WIDTH)
    for g in range(DSA_WIDTH // LANES):
        sl = slice(LANES * g, LANES * (g + 1))
        q8_ref[:, sl] = (rope(qf[:, sl]) * QK_SCALE_LOG2).astype(BF16)
    qif = proj(_E_QI, IDX_HEADS * IDX_DIM)
    for g in range(IDX_HEADS * IDX_DIM // LANES):
        sl = slice(LANES * g, LANES * (g + 1))
        qio_ref[:, sl] = rope(qif[:, sl]).astype(BF16)
    kk = rope(proj(_E_KKI, LANES))
    ka_ref[...] = jnp.where(low, kk, marker).astype(BF16)
    ki_ref[...] = jnp.where(low, pltpu.roll(kk, HEAD_DIM, 1), 0.0).astype(BF16)
    vw = proj(_E_VW, LANES)
    ve = jnp.where(low, vw, marker)
    for c in range(tt // tk):
        vet_ref[c] = jnp.transpose(ve[tk * c:tk * (c + 1), :])[0:VE_ROWS, :].astype(BF16)
    wi_ref[...] = vw * ((IDX_HEADS ** -0.5) * (IDX_DIM ** -0.5))


def _even_front(x, mem_kv, cosf, sinf, w_in, conv_w, conv_b, cln_g, cln_b, pw_w, pw_b, w_out,
                *, batch, seq, mem_len, tt, tk):
    n = x.shape[0]
    nt = seq // tt
    wq = DSA_WIDTH
    wqi = IDX_HEADS * IDX_DIM
    cw = jnp.concatenate([conv_w, jnp.zeros((CONV_HALO - CONV_K, CONV_WIDTH), F32)], axis=0)
    vec = lambda v: v.reshape(1, -1)
    row = lambda b, i: (b * nt + i, 0)
    tile = lambda width: pl.BlockSpec((tt, width), row)
    return pl.pallas_call(
        functools.partial(_even_front_kernel, tt=tt, tk=tk),
        name="even_front",
        grid=(batch, nt),
        in_specs=[tile(D_MODEL),
                  _resident((D_MODEL, _E_TOTAL)),
                  _resident(w_out.shape),
                  pl.BlockSpec((mem_len, 2 * MEM_WIDTH), lambda b, i: (b, 0)),
                  _resident((CONV_HALO, CONV_WIDTH)),
                  _resident((1, CONV_WIDTH)), _resident((1, CONV_WIDTH)), _resident((1, CONV_WIDTH)),
                  _resident((CONV_WIDTH, CONV_WIDTH)),
                  _resident((1, CONV_WIDTH)),
                  tile(LANES), tile(LANES)],
        out_specs=[tile(D_MODEL), tile(DSA_WIDTH), tile(wq), tile(wqi), tile(LANES), tile(LANES),
                   pl.BlockSpec((tt // tk, VE_ROWS, tk), lambda b, i: (b * nt + i, 0, 0)),
                   tile(LANES)],
        out_shape=[jax.ShapeDtypeStruct((n, D_MODEL), F32), jax.ShapeDtypeStruct((n, DSA_WIDTH), F32),
                   jax.ShapeDtypeStruct((n, wq), BF16), jax.ShapeDtypeStruct((n, wqi), BF16),
                   jax.ShapeDtypeStruct((n, LANES), F32), jax.ShapeDtypeStruct((n, LANES), BF16),
                   jax.ShapeDtypeStruct((n // tk, VE_ROWS, tk), BF16), jax.ShapeDtypeStruct((n, LANES), BF16)],
        scratch_shapes=[pltpu.VMEM((CONV_HALO + tt, CONV_WIDTH), F32),
                        pltpu.VMEM((SUBLANES - 1, CONV_HALO + tt - SUBLANES, CONV_WIDTH), F32)],
        compiler_params=_cparams(2),
    )(x, _even_w_in(w_in), w_out.astype(BF16), mem_kv, cw, vec(conv_b), vec(cln_g), vec(cln_b),
      pw_w.astype(BF16), vec(pw_b), cosf, sinf)


def _dsa_kernel(q8_ref, qi_ref, wi_ref, bg_ref, zp_ref, ka_ref, vet_ref, ki_ref, wout_ref, g_ref, b_ref, o_ref,
                keys_ref, hi_ref, lo_ref, bias_ref, s0_ref, s1_ref, p0_ref, p1_ref, acc_ref, m_ref,
                alpha_ref, knorm_ref, *, tq, tk, topk):
    i = pl.program_id(1)
    nkb = (i * tq + tq + tk - 1) // tk
    s_loc = lax.broadcasted_iota(jnp.int32, (tk, tq), 0)
    t_idx = i * tq + lax.broadcasted_iota(jnp.int32, (tk, tq), 1)

    low = lax.broadcasted_iota(jnp.int32, (tq, LANES), 1) < HEAD_DIM

    def split_heads(ref):
        heads = []
        for g in range(ref.shape[1] // LANES):
            two = ref[:, LANES * g:LANES * (g + 1)].astype(F32)
            heads += [jnp.where(low, two, 0.0), jnp.where(low, pltpu.roll(two, HEAD_DIM, 1), 0.0)]
        return heads

    qis = jnp.concatenate([qh.astype(BF16) for qh in split_heads(qi_ref)], axis=0)
    wt = jnp.transpose(wi_ref[...])[HEAD_DIM:HEAD_DIM + SUBLANES, :]

    npairs = (nkb + 1) // 2

    def score_block(j):
        kib = ki_ref[pl.ds(pl.multiple_of(j * tk, tk), tk), :]
        lg = lax.dot_general(kib, qis, _NT, preferred_element_type=F32)
        sc = jnp.maximum(lg[:, 0:tq], 0.0) * wt[0:1, :]
        for h in range(1, IDX_HEADS):
            sc = sc + jnp.maximum(lg[:, h * tq:(h + 1) * tq], 0.0) * wt[h:h + 1, :]
        bits = pltpu.bitcast(sc, jnp.int32)
        key = bits ^ ((bits >> 31) & 0x7FFFFFFF)
        key = jnp.where(j * tk + s_loc <= t_idx, key, INT_MIN)
        keys_ref[j] = key
        hi_ref[j] = (key >> 16).astype(jnp.int16)
        lo_ref[j] = ((key & 0xFFFF) - HALF16).astype(jnp.int16)

    def score_pair(t, carry):
        score_block(2 * t)
        score_block(2 * t + 1)
        return carry

    lax.fori_loop(0, npairs, score_pair, 0)

    nacc = tk // 4
    one, zero = jnp.asarray(1.0, BF16), jnp.asarray(0.0, BF16)

    def count16(d_ref, pred_fn):
        def chunks(j):
            r = jnp.where(pred_fn(d_ref[j]), one, zero).reshape(4, nacc, tq)
            return (r[0] + r[1]) + (r[2] + r[3])

        def pair(t, acc):
            return acc + (chunks(2 * t) + chunks(2 * t + 1))
        acc = lax.fori_loop(0, npairs, pair, jnp.zeros((nacc, tq), BF16))
        return jnp.sum(acc.astype(F32), axis=0, keepdims=True)

    def digit_search(d_ref, target):
        def bit_step(b, prefix):
            cand = prefix | lax.shift_left(jnp.int32(1), 15 - b)
            cand_s = (cand - HALF16).astype(jnp.int16)
            cnt = count16(d_ref, lambda d: d >= cand_s)
            return jnp.where(cnt >= target, cand, prefix)
        return lax.fori_loop(0, 16, bit_step, jnp.zeros((1, tq), jnp.int32))

    thr_hi = digit_search(hi_ref, topk) - HALF16
    thr_hi16 = thr_hi.astype(jnp.int16)
    c_hi = count16(hi_ref, lambda d: d > thr_hi16)

    def low_digit_pair(t, carry):
        for j in (2 * t, 2 * t + 1):
            lo_ref[j] = jnp.where(hi_ref[j] == thr_hi16, lo_ref[j], jnp.asarray(-HALF16, jnp.int16))
        return carry

    lax.fori_loop(0, npairs, low_digit_pair, 0)
    thr_lo = digit_search(lo_ref, topk - c_hi)
    thr_lo16 = (thr_lo - HALF16).astype(jnp.int16)
    c_gt = c_hi + count16(lo_ref, lambda d: d > thr_lo16)
    thr = lax.shift_left(thr_hi, 16) | thr_lo
    n_tie = jnp.where(thr == INT_MIN, 0.0, topk - c_gt)
    c_eq = count16(lo_ref, lambda d: d == thr_lo16)
    excess = jnp.max(jnp.where(thr == INT_MIN, 0.0, c_eq - n_tie))
    some_excess = excess > 0.5

    @pl.when(some_excess)
    def _():
        lower = jnp.where(lax.broadcasted_iota(jnp.int32, (tk, tk), 1)
                          < lax.broadcasted_iota(jnp.int32, (tk, tk), 0), 1.0, 0.0).astype(BF16)

        def select_block(j, off):
            kb = keys_ref[j]
            tie = kb == thr
            tie_f = jnp.where(tie, 1.0, 0.0)
            before = jnp.dot(lower, tie_f.astype(BF16), preferred_element_type=F32) + off
            bias_ref[j] = jnp.where(kb > thr, 0.0,
                                    jnp.where(tie, jnp.where(before < n_tie, 0.0, NEG), NEG))
            return off + jnp.sum(tie_f, axis=0, keepdims=True)

        def select_pair(t, off):
            return select_block(2 * t + 1, select_block(2 * t, off))

        lax.fori_loop(0, npairs, select_pair, jnp.zeros((1, tq), F32))

    @pl.when(jnp.logical_not(some_excess))
    def _():
        floor = jnp.where(thr == INT_MIN, INT_MIN + 1, thr)

        def select_pair(t, carry):
            for j in (2 * t, 2 * t + 1):
                bias_ref[j] = jnp.where(keys_ref[j] >= floor, 0.0, NEG)
            return carry

        lax.fori_loop(0, npairs, select_pair, 0)

    @pl.when(i == 0)
    def _():
        def block_norm(j, best):
            kf = ka_ref[pl.ds(pl.multiple_of(j * tk, tk), tk), :].astype(F32)
            return jnp.maximum(best, jnp.max(jnp.sum(kf * kf, axis=1, keepdims=True), axis=0, keepdims=True))
        k2 = lax.fori_loop(0, ka_ref.shape[0] // tk, block_norm, jnp.zeros((1, 1), F32))
        knorm_ref[...] = jnp.broadcast_to(k2, knorm_ref.shape)

    kmax = jnp.sqrt(knorm_ref[0:1, 0:1]) * BOUND_MARGIN
    slope = kmax * (-0.5 / Q_NORM_GUESS)
    offset = kmax * (-0.5 * Q_NORM_GUESS)
    shift_lane = lax.broadcasted_iota(jnp.int32, (tq, LANES), 1) == HEAD_DIM
    plain, shifted = [], []
    for qf in split_heads(q8_ref):
        neg_bound = jnp.sum(qf * qf, axis=1, keepdims=True) * slope + offset
        plain.append(qf.astype(BF16))
        shifted.append(jnp.where(shift_lane, neg_bound, qf).astype(BF16))
    qs_plain = jnp.concatenate(plain, axis=0)
    qs = jnp.concatenate(shifted, axis=0)
    acc_ref[...] = jnp.zeros(acc_ref.shape, F32)

    last = 2 * npairs - 1

    def masked_scores(q_stack, jc):
        rows = pl.ds(pl.multiple_of(jc * tk, tk), tk)
        s = lax.dot_general(ka_ref[rows, :], q_stack, _NT, preferred_element_type=F32)
        return [s[:, h * tq:(h + 1) * tq] + bias_ref[jc] for h in range(DSA_HEADS)]

    def scores_stage(s_ref, jb):
        for h, sh in enumerate(masked_scores(qs, jnp.minimum(jb, last))):
            s_ref[:, h * tq:(h + 1) * tq] = sh

    def exp_stage(s_ref, p_ref):
        for h in range(DSA_HEADS):
            hs = slice(h * tq, (h + 1) * tq)
            p_ref[:, hs] = jnp.exp2(s_ref[:, hs]).astype(BF16)

    def value_stage(p_ref, jb):
        acc_ref[...] += jnp.dot(vet_ref[jnp.clip(jb, 0, last)], p_ref[...], preferred_element_type=F32)

    scores_stage(s0_ref, 0)
    p1_ref[...] = jnp.zeros(p1_ref.shape, BF16)

    def attend_pair(t, carry):
        j = 2 * t
        scores_stage(s1_ref, j + 1)
        exp_stage(s0_ref, p0_ref)
        value_stage(p1_ref, j - 1)
        scores_stage(s0_ref, j + 2)
        exp_stage(s1_ref, p1_ref)
        value_stage(p0_ref, j)
        return carry

    lax.fori_loop(0, npairs, attend_pair, 0)
    value_stage(p1_ref, last)

    denom = acc_ref[HEAD_DIM:HEAD_DIM + 1, :]
    underflow = jnp.max(jnp.where(denom > TINY_DENOM, 0.0, 1.0)) > 0.5

    @pl.when(underflow)
    def _():
        m_ref[...] = jnp.full(m_ref.shape, NEG, F32)
        acc_ref[...] = jnp.zeros(acc_ref.shape, F32)

        def exact_block(j, carry):
            for h, sh in enumerate(masked_scores(qs_plain, j)):
                hs = slice(h * tq, (h + 1) * tq)
                m_old = m_ref[h:h + 1, :]
                m_new = jnp.maximum(m_old, jnp.max(sh, axis=0, keepdims=True))
                alpha_ref[h:h + 1, :] = jnp.exp2(m_old - m_new)
                m_ref[h:h + 1, :] = m_new
                p0_ref[:, hs] = jnp.exp2(sh - m_new).astype(BF16)
            pv = jnp.dot(vet_ref[j], p0_ref[...], preferred_element_type=F32)
            for h in range(DSA_HEADS):
                hs = slice(h * tq, (h + 1) * tq)
                acc_ref[:, hs] = acc_ref[:, hs] * alpha_ref[h:h + 1, :] + pv[:, hs]
            return carry

        lax.fori_loop(0, 2 * npairs, exact_block, 0)

    def head_out(h):
        a = acc_ref[:, h * tq:(h + 1) * tq]
        return a[0:HEAD_DIM, :] / a[HEAD_DIM:HEAD_DIM + 1, :]

    pairs = [jnp.transpose(jnp.concatenate([head_out(2 * g), head_out(2 * g + 1)], axis=0))
             for g in range(DSA_HEADS // 2)]
    yb = jnp.concatenate(pairs, axis=1) * _silu(bg_ref[...])
    z = zp_ref[...] + jnp.dot(yb.astype(BF16), wout_ref[...], preferred_element_type=F32)
    o_ref[...] = _layer_norm(z, g_ref[...], b_ref[...])


def _dsa(q8, qi, wi, bg, zp, ka, vet, ki, w_out_b, ln_g, ln_b, *, batch, seq, tq, tk):
    n = q8.shape[0]
    nq = seq // tq
    nk = seq // tk
    topk = min(TOPK_MAX, seq // 4)
    qrow = lambda b, i: (b * nq + i, 0)
    krow = lambda b, i: (b, 0)
    qtile = lambda width: pl.BlockSpec((tq, width), qrow)
    kernel = functools.partial(_dsa_kernel, tq=tq, tk=tk, topk=float(topk))
    return pl.pallas_call(
        kernel,
        name="dsa_attention",
        grid=(batch, nq),
        in_specs=[qtile(DSA_WIDTH), qtile(IDX_HEADS * IDX_DIM), qtile(LANES), qtile(DSA_WIDTH),
                  qtile(D_MODEL),
                  pl.BlockSpec((seq, LANES), krow),
                  pl.BlockSpec((nk, VE_ROWS, tk), lambda b, i: (b, 0, 0)),
                  pl.BlockSpec((seq, LANES), krow),
                  _resident((DSA_WIDTH, D_MODEL)), _resident((1, D_MODEL)), _resident((1, D_MODEL))],
        out_specs=qtile(D_MODEL),
        out_shape=jax.ShapeDtypeStruct((n, D_MODEL), F32),
        scratch_shapes=[pltpu.VMEM((nk, tk, tq), jnp.int32),
                        pltpu.VMEM((nk, tk, tq), jnp.int16),
                        pltpu.VMEM((nk, tk, tq), jnp.int16),
                        pltpu.VMEM((nk, tk, tq), F32),
                        pltpu.VMEM((tk, DSA_HEADS * tq), F32),
                        pltpu.VMEM((tk, DSA_HEADS * tq), F32),
                        pltpu.VMEM((tk, DSA_HEADS * tq), BF16),
                        pltpu.VMEM((tk, DSA_HEADS * tq), BF16),
                        pltpu.VMEM((VE_ROWS, DSA_HEADS * tq), F32),
                        pltpu.VMEM((DSA_HEADS, tq), F32),
                        pltpu.VMEM((DSA_HEADS, tq), F32),
                        pltpu.VMEM((SUBLANES, LANES), F32)],
        compiler_params=_cparams(2),
    )(q8, qi, wi, bg, zp, ka, vet, ki, w_out_b, ln_g.reshape(1, D_MODEL), ln_b.reshape(1, D_MODEL))


def _odd_kernel(x_ref, win_ref, wout_ref, kv_ref, vg_ref, vb_ref, ws_ref, bs_ref, g_ref, b_ref, o_ref,
                vn_ref, ug_ref, y_ref, *, tt):
    x = x_ref[...]
    xb = x.astype(BF16)

    def proj(col, width):
        return jnp.dot(xb, win_ref[:, col:col + width], preferred_element_type=F32)

    vn_ref[...] = _layer_norm(_gelu_tanh(proj(SG_WIDTH, SG_WIDTH)), vg_ref[...], vb_ref[...]).astype(BF16)
    ug_ref[...] = _gelu_tanh(proj(0, SG_WIDTH)) * _silu(proj(2 * SG_WIDTH, SG_WIDTH))
    causal = (lax.broadcasted_iota(jnp.int32, (SG_CHUNK, SG_CHUNK), 1)
              <= lax.broadcasted_iota(jnp.int32, (SG_CHUNK, SG_CHUNK), 0))
    gw = SG_WIDTH // SG_GROUPS
    for g in range(SG_GROUPS):
        w = jnp.where(causal, ws_ref[g], 0.0).astype(BF16)
        bias = bs_ref[:, g:g + 1]
        cs = slice(gw * g, gw * (g + 1))
        for c in range(tt // SG_CHUNK):
            rs = slice(SG_CHUNK * c, SG_CHUNK * (c + 1))
            mixed = jnp.dot(w, vn_ref[rs, cs], preferred_element_type=F32) + bias
            y_ref[rs, cs] = (ug_ref[rs, cs] * mixed).astype(BF16)
    z = DN_ALPHA * x + jnp.dot(y_ref[...], wout_ref[0:SG_WIDTH, :], preferred_element_type=F32)
    ym = _mem_heads(proj(3 * SG_WIDTH, MEM_WIDTH), kv_ref) * _silu(proj(3 * SG_WIDTH + MEM_WIDTH, MEM_WIDTH))
    z = z + jnp.dot(ym.astype(BF16), wout_ref[SG_WIDTH:SG_WIDTH + MEM_WIDTH, :], preferred_element_type=F32)
    o_ref[...] = _layer_norm(z, g_ref[...], b_ref[...])


def _odd_layer(x, mem_kv, vln_g, vln_b, ws, bs, w_in, w_out, ln_g, ln_b, *, seq, mem_len, tt):
    n = x.shape[0]
    nt = seq // tt
    row = lambda i: (i, 0)
    vec = lambda v: v.reshape(1, -1)
    return pl.pallas_call(
        functools.partial(_odd_kernel, tt=tt),
        name="odd_layer",
        grid=(n // tt,),
        in_specs=[pl.BlockSpec((tt, D_MODEL), row),
                  _resident(w_in.shape),
                  _resident(w_out.shape),
                  pl.BlockSpec((mem_len, 2 * MEM_WIDTH), lambda i: (i // nt, 0)),
                  _resident((1, SG_WIDTH)), _resident((1, SG_WIDTH)),
                  _resident((SG_GROUPS, SG_CHUNK, SG_CHUNK)),
                  _resident((SG_CHUNK, SG_GROUPS)),
                  _resident((1, D_MODEL)), _resident((1, D_MODEL))],
        out_specs=pl.BlockSpec((tt, D_MODEL), row),
        out_shape=jax.ShapeDtypeStruct((n, D_MODEL), F32),
        scratch_shapes=[pltpu.VMEM((tt, SG_WIDTH), BF16),
                        pltpu.VMEM((tt, SG_WIDTH), F32),
                        pltpu.VMEM((tt, SG_WIDTH), BF16)],
        compiler_params=_cparams(1),
    )(x, w_in.astype(BF16), w_out.astype(BF16), mem_kv, vec(vln_g), vec(vln_b), ws, jnp.transpose(bs),
      vec(ln_g), vec(ln_b))


def _even_layer(x, mem_kv, cosf, sinf, w_in, conv_w, conv_b, cln_g, cln_b, pw2_w, pw2_b, w_out, ln_g, ln_b,
                *, batch, seq, mem_len, tt):
    tk = min(K_TILE, seq)
    zp, bg, q8, qi, wi, ka, vet, ki = _even_front(x, mem_kv, cosf, sinf, w_in, conv_w, conv_b, cln_g, cln_b,
                                                  pw2_w, pw2_b, w_out, batch=batch, seq=seq, mem_len=mem_len,
                                                  tt=tt, tk=tk)
    w_out_b = w_out[CONV_WIDTH:CONV_WIDTH + DSA_WIDTH].astype(BF16)
    return _dsa(q8, qi, wi, bg, zp, ka, vet, ki, w_out_b, ln_g, ln_b, batch=batch, seq=seq, tq=Q_TILE, tk=tk)


def kernel(x, mem, positions, e_w_in, e_conv_w, e_conv_b, e_cln_g, e_cln_b, e_pw2_w, e_pw2_b, e_w_out, o_w_in, o_vln_g, o_vln_b, o_ws, o_bs, o_w_out, mem_wk, mem_wv, ln_g, ln_b):
    batch, seq, d = x.shape
    mem_len = mem.shape[1]
    tt = min(TOKEN_TILE, seq)
    cosf, sinf = _rope_tables(positions)
    xf = x.reshape(batch * seq, d)
    memf = mem.reshape(batch * mem_len, d)
    for layer in range(DEPTH):
        j = layer // 2
        w_kv = jnp.concatenate([mem_wk[layer], mem_wv[layer]], axis=1).astype(BF16)
        mem_kv = _matmul(memf, w_kv, tm=min(tt, memf.shape[0]), tn=1024, out_dtype=BF16)
        if layer % 2 == 0:
            xf = _even_layer(xf, mem_kv, cosf, sinf, e_w_in[j], e_conv_w[j], e_conv_b[j], e_cln_g[j], e_cln_b[j],
                             e_pw2_w[j], e_pw2_b[j], e_w_out[j], ln_g[layer], ln_b[layer],
                             batch=batch, seq=seq, mem_len=mem_len, tt=tt)
        else:
            xf = _odd_layer(xf, mem_kv, o_vln_g[j], o_vln_b[j], o_ws[j], o_bs[j], o_w_in[j], o_w_out[j],
                            ln_g[layer], ln_b[layer], seq=seq, mem_len=mem_len, tt=tt)
    return xf.reshape(batch, seq, d)
```

```python
import functools

import numpy as np
import jax
import jax.numpy as jnp
from jax import lax
from jax.experimental import pallas as pl
from jax.experimental.pallas import tpu as pltpu

F32 = jnp.float32
BF16 = jnp.bfloat16

D_MODEL = 1024
DEPTH = 4
HEAD_DIM = 64
CONV_WIDTH = 512
CONV_K = 31
DSA_HEADS = 8
DSA_WIDTH = 512
IDX_HEADS = 4
IDX_DIM = 64
TOPK_MAX = 256
SG_CHUNK = 128
SG_GROUPS = 8
SG_WIDTH = 1024
MEM_HEADS = 4
MEM_HEAD_DIM = 128
MEM_WIDTH = 512
ROPE_THETA = 10000.0
LN_EPS = 1e-5
DN_ALPHA = (2 * DEPTH) ** 0.25

LANES = 128
SUBLANES = 8
VE_ROWS = HEAD_DIM + 16
CONV_HALO = 32
NEG = -1e30
INT_MIN = -(2 ** 31)
HALF16 = 2 ** 15
QK_SCALE_LOG2 = float(np.log2(np.e)) * HEAD_DIM ** -0.5
BOUND_MARGIN = 1.0 + 2.0 ** -8
Q_NORM_GUESS = QK_SCALE_LOG2 * HEAD_DIM ** 0.5
TINY_DENOM = 2.0 ** -100

VMEM_LIMIT = 56 * 1024 * 1024

TOKEN_TILE = 512
Q_TILE = 256
K_TILE = 256

_NT = (((1,), (1,)), ((), ()))


def _cparams(n_axes, flags=None):
    return pltpu.CompilerParams(dimension_semantics=("arbitrary",) * n_axes,
                                vmem_limit_bytes=VMEM_LIMIT, flags=flags)


def _resident(shape):
    zeros = (0,) * len(shape)
    return pl.BlockSpec(shape, lambda *_: zeros, pipeline_mode=pl.Buffered(1))


def _silu(g):
    return g * (1.0 / (1.0 + jnp.exp(-g)))


def _gelu_tanh(x):
    c = np.float32(np.sqrt(2.0 / np.pi))
    return x * (0.5 * (1.0 + jnp.tanh(c * (x + 0.044715 * (x * x * x)))))


def _layer_norm(x, g, b):
    mu = jnp.mean(x, axis=-1, keepdims=True)
    xc = x - mu
    var = jnp.mean(xc * xc, axis=-1, keepdims=True)
    return xc * lax.rsqrt(var + LN_EPS) * g + b


def _mem_heads(mq, kv_ref):
    outs = []
    for h in range(MEM_HEADS):
        hs = slice(MEM_HEAD_DIM * h, MEM_HEAD_DIM * (h + 1))
        vs = slice(MEM_WIDTH + MEM_HEAD_DIM * h, MEM_WIDTH + MEM_HEAD_DIM * (h + 1))
        s = lax.dot_general(mq[:, hs].astype(BF16), kv_ref[:, hs], _NT,
                            preferred_element_type=F32) * (MEM_HEAD_DIM ** -0.5)
        e = jnp.exp(s - jnp.max(s, axis=-1, keepdims=True))
        p = e / jnp.sum(e, axis=-1, keepdims=True)
        outs.append(jnp.dot(p.astype(BF16), kv_ref[:, vs], preferred_element_type=F32))
    return jnp.concatenate(outs, axis=1)


def _mm_kernel(x_ref, w_ref, o_ref):
    o_ref[...] = jnp.dot(x_ref[...].astype(BF16), w_ref[...],
                         preferred_element_type=F32).astype(o_ref.dtype)


def _matmul(x, w, *, tm, tn, out_dtype=F32):
    m, k = x.shape
    n = w.shape[1]
    return pl.pallas_call(
        _mm_kernel,
        name="mem_kv_proj",
        grid=(m // tm, n // tn),
        in_specs=[pl.BlockSpec((tm, k), lambda i, j: (i, 0)),
                  pl.BlockSpec((k, tn), lambda i, j: (0, j))],
        out_specs=pl.BlockSpec((tm, tn), lambda i, j: (i, j)),
        out_shape=jax.ShapeDtypeStruct((m, n), out_dtype),
        compiler_params=_cparams(2),
    )(x, w)


def _rope_table_kernel(pos_ref, inv_ref, cos_ref, sin_ref):
    ang = pos_ref[...].astype(F32) * inv_ref[...]
    half = HEAD_DIM // 2
    groups = LANES // half
    grp = lax.broadcasted_iota(jnp.int32, ang.shape, 1) // half
    sign = jnp.where((grp & 1) == 0, -1.0, 1.0)
    for table, out_ref, scale in ((jnp.cos(ang), cos_ref, None), (jnp.sin(ang), sin_ref, sign)):
        rolled = [table] + [pltpu.roll(table, half * k, 1) for k in range(1, groups)]
        for q in range(groups):
            y = rolled[(0 - q) % groups]
            for g in range(1, groups):
                y = jnp.where(grp == g, rolled[(g - q) % groups], y)
            out_ref[q] = y if scale is None else y * scale


def _rope_tables(positions):
    n = positions.size
    half = HEAD_DIM // 2
    per_row = LANES // half
    rows = n // per_row
    inv = (ROPE_THETA ** (-jnp.arange(0, HEAD_DIM, 2, dtype=F32) / HEAD_DIM)).astype(F32)
    pos_rep = jnp.repeat(jnp.transpose(positions.reshape(per_row, rows)), half, axis=1)
    inv_rep = jnp.tile(inv, per_row).reshape(1, LANES)
    tr = min(512, rows)
    cos, sin = pl.pallas_call(
        _rope_table_kernel,
        name="rope_tables",
        grid=(rows // tr,),
        in_specs=[pl.BlockSpec((tr, LANES), lambda i: (i, 0)),
                  pl.BlockSpec((1, LANES), lambda i: (0, 0))],
        out_specs=[pl.BlockSpec((per_row, tr, LANES), lambda i: (0, i, 0))] * 2,
        out_shape=[jax.ShapeDtypeStruct((per_row, rows, LANES), F32)] * 2,
        compiler_params=_cparams(1),
    )(pos_rep, inv_rep)
    return cos.reshape(n, LANES), sin.reshape(n, LANES)


_E_GATE_B = 3 * CONV_WIDTH
_E_MQ = _E_GATE_B + DSA_WIDTH
_E_MGATE = _E_MQ + MEM_WIDTH
_E_Q = _E_MGATE + MEM_WIDTH
_E_QI = _E_Q + DSA_WIDTH
_E_KKI = _E_QI + IDX_HEADS * IDX_DIM
_E_VW = _E_KKI + LANES
_E_TOTAL = _E_VW + LANES


_E_SPLIT_NAMES = ("a_val", "a_glu", "a_gate", "q", "k", "v", "qi", "ki", "wi", "b_gate", "mq", "m_gate")
_E_SPLIT_SIZES = (CONV_WIDTH, CONV_WIDTH, CONV_WIDTH, DSA_WIDTH, HEAD_DIM, HEAD_DIM,
                  IDX_HEADS * IDX_DIM, IDX_DIM, IDX_HEADS, DSA_WIDTH, MEM_WIDTH, MEM_WIDTH)
_E_SPLIT = dict(zip(_E_SPLIT_NAMES, zip(np.cumsum((0,) + _E_SPLIT_SIZES[:-1]).tolist(), _E_SPLIT_SIZES)))
_E_ORDER = ("a_val", "a_glu", "a_gate", "b_gate", "mq", "m_gate", "q", "qi", "k", "ki", "v", "wi")


def _even_w_layout_kernel(w_ref, o_ref):
    col = 0
    for name in _E_ORDER:
        src, width = _E_SPLIT[name]
        o_ref[:, col:col + width] = w_ref[:, src:src + width].astype(BF16)
        col += width
    o_ref[:, col:_E_TOTAL] = jnp.zeros((o_ref.shape[0], _E_TOTAL - col), BF16)


def _even_w_in(w_stack, layer):
    _, d, n_in = w_stack.shape
    tr = 256
    return pl.pallas_call(
        _even_w_layout_kernel,
        name="even_w_layout",
        grid=(d // tr,),
        in_specs=[pl.BlockSpec((None, tr, n_in), lambda i: (layer, i, 0))],
        out_specs=pl.BlockSpec((tr, _E_TOTAL), lambda i: (i, 0)),
        out_shape=jax.ShapeDtypeStruct((d, _E_TOTAL), BF16),
        compiler_params=_cparams(1),
    )(w_stack)


def _even_front_kernel(x_ref, win_ref, wout_ref, kv_ref, cw_ref, cb_ref, cg_ref, cbeta_ref, pw_ref, pb_ref,
                       cos_ref, sin_ref,
                       zp_ref, bg_ref, q8_ref, qio_ref, wi_ref, ka_ref, vet_ref, ki_ref,
                       hist_ref, shift_ref, *, tt, tk):
    @pl.when(pl.program_id(1) == 0)
    def _():
        hist_ref[0:CONV_HALO, :] = jnp.zeros((CONV_HALO, CONV_WIDTH), F32)

    x = x_ref[...]
    xb = x.astype(BF16)

    def proj(col, width):
        return jnp.dot(xb, win_ref[:, col:col + width], preferred_element_type=F32)

    glu = proj(CONV_WIDTH, CONV_WIDTH)
    hist_ref[CONV_HALO:CONV_HALO + tt, :] = proj(0, CONV_WIDTH) * (1.0 / (1.0 + jnp.exp(-glu)))
    span = tt + CONV_HALO - SUBLANES
    for r in range(1, SUBLANES):
        shift_ref[r - 1, 0:span, :] = hist_ref[pl.ds(r, span), :]
    base = CONV_HALO - (CONV_K - 1)
    acc = None
    for j in range(CONV_K):
        a, r = divmod(base + j, SUBLANES)
        rows = slice(SUBLANES * a, SUBLANES * a + tt)
        src = hist_ref[rows, :] if r == 0 else shift_ref[r - 1, rows, :]
        term = src * cw_ref[j:j + 1, :]
        acc = term if acc is None else acc + term
    acc = acc + cb_ref[...]
    hist_ref[0:CONV_HALO, :] = hist_ref[tt:tt + CONV_HALO, :]
    y = _silu(_layer_norm(acc, cg_ref[...], cbeta_ref[...]))
    y = jnp.dot(y.astype(BF16), pw_ref[...], preferred_element_type=F32) + pb_ref[...]
    ya = y * _silu(proj(2 * CONV_WIDTH, CONV_WIDTH))
    z = DN_ALPHA * x + jnp.dot(ya.astype(BF16), wout_ref[0:CONV_WIDTH, :], preferred_element_type=F32)

    ym = _mem_heads(proj(_E_MQ, MEM_WIDTH), kv_ref) * _silu(proj(_E_MGATE, MEM_WIDTH))
    m0 = CONV_WIDTH + DSA_WIDTH
    zp_ref[...] = z + jnp.dot(ym.astype(BF16), wout_ref[m0:m0 + MEM_WIDTH, :], preferred_element_type=F32)

    bg_ref[...] = proj(_E_GATE_B, DSA_WIDTH)
    cosf = cos_ref[...]
    sinf = sin_ref[...]

    lane = lax.broadcasted_iota(jnp.int32, (tt, LANES), 1)
    first_half = (lane & (HEAD_DIM - 1)) < HEAD_DIM // 2
    low = lane < HEAD_DIM
    marker = jnp.where(lane == HEAD_DIM, 1.0, 0.0)

    def rope(g):
        rot = jnp.where(first_half, pltpu.roll(g, LANES - HEAD_DIM // 2, 1), pltpu.roll(g, HEAD_DIM // 2, 1))
        return g * cosf + rot * sinf

    qf = proj(_E_Q, DSA_WIDTH)
    for g in range(DSA_WIDTH // LANES):
        sl = slice(LANES * g, LANES * (g + 1))
        q8_ref[:, sl] = (rope(qf[:, sl]) * QK_SCALE_LOG2).astype(BF16)
    qif = proj(_E_QI, IDX_HEADS * IDX_DIM)
    for g in range(IDX_HEADS * IDX_DIM // LANES):
        sl = slice(LANES * g, LANES * (g + 1))
        qio_ref[:, sl] = rope(qif[:, sl]).astype(BF16)
    kk = rope(proj(_E_KKI, LANES))
    ka_ref[...] = jnp.where(low, kk, marker).astype(BF16)
    ki_ref[...] = jnp.where(low, pltpu.roll(kk, HEAD_DIM, 1), 0.0).astype(BF16)
    vw = proj(_E_VW, LANES)
    ve = jnp.where(low, vw, marker)
    for c in range(tt // tk):
        vet_ref[c] = jnp.transpose(ve[tk * c:tk * (c + 1), :])[0:VE_ROWS, :].astype(BF16)
    wi_ref[...] = vw * ((IDX_HEADS ** -0.5) * (IDX_DIM ** -0.5))


def _even_front(x, mem_kv, cosf, sinf, w_in, conv_w, conv_b, cln_g, cln_b, pw_w, pw_b, w_out,
                *, batch, seq, mem_len, tt, tk):
    n = x.shape[0]
    nt = seq // tt
    wq = DSA_WIDTH
    wqi = IDX_HEADS * IDX_DIM
    cw = jnp.concatenate([conv_w, jnp.zeros((CONV_HALO - CONV_K, CONV_WIDTH), F32)], axis=0)
    vec = lambda v: v.reshape(1, -1)
    row = lambda b, i: (b * nt + i, 0)
    tile = lambda width: pl.BlockSpec((tt, width), row)
    return pl.pallas_call(
        functools.partial(_even_front_kernel, tt=tt, tk=tk),
        name="even_front",
        grid=(batch, nt),
        in_specs=[tile(D_MODEL),
                  _resident((D_MODEL, _E_TOTAL)),
                  _resident(w_out.shape),
                  pl.BlockSpec((mem_len, 2 * MEM_WIDTH), lambda b, i: (b, 0)),
                  _resident((CONV_HALO, CONV_WIDTH)),
                  _resident((1, CONV_WIDTH)), _resident((1, CONV_WIDTH)), _resident((1, CONV_WIDTH)),
                  _resident((CONV_WIDTH, CONV_WIDTH)),
                  _resident((1, CONV_WIDTH)),
                  tile(LANES), tile(LANES)],
        out_specs=[tile(D_MODEL), tile(DSA_WIDTH), tile(wq), tile(wqi), tile(LANES), tile(LANES),
                   pl.BlockSpec((tt // tk, VE_ROWS, tk), lambda b, i: (b * nt + i, 0, 0)),
                   tile(LANES)],
        out_shape=[jax.ShapeDtypeStruct((n, D_MODEL), F32), jax.ShapeDtypeStruct((n, DSA_WIDTH), F32),
                   jax.ShapeDtypeStruct((n, wq), BF16), jax.ShapeDtypeStruct((n, wqi), BF16),
                   jax.ShapeDtypeStruct((n, LANES), F32), jax.ShapeDtypeStruct((n, LANES), BF16),
                   jax.ShapeDtypeStruct((n // tk, VE_ROWS, tk), BF16), jax.ShapeDtypeStruct((n, LANES), BF16)],
        scratch_shapes=[pltpu.VMEM((CONV_HALO + tt, CONV_WIDTH), F32),
                        pltpu.VMEM((SUBLANES - 1, CONV_HALO + tt - SUBLANES, CONV_WIDTH), F32)],
        compiler_params=_cparams(2),
    )(x, w_in, w_out.astype(BF16), mem_kv, cw, vec(conv_b), vec(cln_g), vec(cln_b),
      pw_w.astype(BF16), vec(pw_b), cosf, sinf)


def _dsa_kernel(q8_ref, qi_ref, wi_ref, bg_ref, zp_ref, ka_ref, vet_ref, ki_ref, wout_ref, g_ref, b_ref, o_ref,
                keys_ref, hi_ref, lo_ref, bias_ref, s0_ref, s1_ref, p0_ref, p1_ref, acc_ref, m_ref,
                alpha_ref, knorm_ref, *, tq, tk, topk):
    i = pl.program_id(1)
    nkb = (i * tq + tq + tk - 1) // tk
    s_loc = lax.broadcasted_iota(jnp.int32, (tk, tq), 0)
    t_idx = i * tq + lax.broadcasted_iota(jnp.int32, (tk, tq), 1)

    low = lax.broadcasted_iota(jnp.int32, (tq, LANES), 1) < HEAD_DIM

    def split_heads(ref):
        heads = []
        for g in range(ref.shape[1] // LANES):
            two = ref[:, LANES * g:LANES * (g + 1)].astype(F32)
            heads += [jnp.where(low, two, 0.0), jnp.where(low, pltpu.roll(two, HEAD_DIM, 1), 0.0)]
        return heads

    qis = jnp.concatenate([qh.astype(BF16) for qh in split_heads(qi_ref)], axis=0)
    wt = jnp.transpose(wi_ref[...])[HEAD_DIM:HEAD_DIM + SUBLANES, :]

    npairs = (nkb + 1) // 2

    def score_block(j):
        kib = ki_ref[pl.ds(pl.multiple_of(j * tk, tk), tk), :]
        lg = lax.dot_general(kib, qis, _NT, preferred_element_type=F32)
        sc = jnp.maximum(lg[:, 0:tq], 0.0) * wt[0:1, :]
        for h in range(1, IDX_HEADS):
            sc = sc + jnp.maximum(lg[:, h * tq:(h + 1) * tq], 0.0) * wt[h:h + 1, :]
        bits = pltpu.bitcast(sc, jnp.int32)
        key = bits ^ ((bits >> 31) & 0x7FFFFFFF)
        key = jnp.where(j * tk + s_loc <= t_idx, key, INT_MIN)
        keys_ref[j] = key
        hi_ref[j] = (key >> 16).astype(jnp.int16)
        lo_ref[j] = ((key & 0xFFFF) - HALF16).astype(jnp.int16)

    def score_pair(t, carry):
        score_block(2 * t)
        score_block(2 * t + 1)
        return carry

    lax.fori_loop(0, npairs, score_pair, 0)

    nacc = tk // 4
    one, zero = jnp.asarray(1.0, BF16), jnp.asarray(0.0, BF16)

    def count16(d_ref, pred_fn):
        def chunks(j):
            r = jnp.where(pred_fn(d_ref[j]), one, zero).reshape(4, nacc, tq)
            return (r[0] + r[1]) + (r[2] + r[3])

        def pair(t, acc):
            return acc + (chunks(2 * t) + chunks(2 * t + 1))
        acc = lax.fori_loop(0, npairs, pair, jnp.zeros((nacc, tq), BF16))
        return jnp.sum(acc.astype(F32), axis=0, keepdims=True)

    def digit_search(d_ref, target):
        def bit_step(b, prefix):
            cand = prefix | lax.shift_left(jnp.int32(1), 15 - b)
            cand_s = (cand - HALF16).astype(jnp.int16)
            cnt = count16(d_ref, lambda d: d >= cand_s)
            return jnp.where(cnt >= target, cand, prefix)
        return lax.fori_loop(0, 16, bit_step, jnp.zeros((1, tq), jnp.int32))

    thr_hi = digit_search(hi_ref, topk) - HALF16
    thr_hi16 = thr_hi.astype(jnp.int16)
    c_hi = count16(hi_ref, lambda d: d > thr_hi16)

    def low_digit_pair(t, carry):
        for j in (2 * t, 2 * t + 1):
            lo_ref[j] = jnp.where(hi_ref[j] == thr_hi16, lo_ref[j], jnp.asarray(-HALF16, jnp.int16))
        return carry

    lax.fori_loop(0, npairs, low_digit_pair, 0)
    thr_lo = digit_search(lo_ref, topk - c_hi)
    thr_lo16 = (thr_lo - HALF16).astype(jnp.int16)
    c_gt = c_hi + count16(lo_ref, lambda d: d > thr_lo16)
    thr = lax.shift_left(thr_hi, 16) | thr_lo
    n_tie = jnp.where(thr == INT_MIN, 0.0, topk - c_gt)
    c_eq = count16(lo_ref, lambda d: d == thr_lo16)
    excess = jnp.max(jnp.where(thr == INT_MIN, 0.0, c_eq - n_tie))
    some_excess = excess > 0.5

    @pl.when(some_excess)
    def _():
        lower = jnp.where(lax.broadcasted_iota(jnp.int32, (tk, tk), 1)
                          < lax.broadcasted_iota(jnp.int32, (tk, tk), 0), 1.0, 0.0).astype(BF16)

        def select_block(j, off):
            kb = keys_ref[j]
            tie = kb == thr
            tie_f = jnp.where(tie, 1.0, 0.0)
            before = jnp.dot(lower, tie_f.astype(BF16), preferred_element_type=F32) + off
            bias_ref[j] = jnp.where(kb > thr, 0.0,
                                    jnp.where(tie, jnp.where(before < n_tie, 0.0, NEG), NEG))
            return off + jnp.sum(tie_f, axis=0, keepdims=True)

        def select_pair(t, off):
            return select_block(2 * t + 1, select_block(2 * t, off))

        lax.fori_loop(0, npairs, select_pair, jnp.zeros((1, tq), F32))

    @pl.when(jnp.logical_not(some_excess))
    def _():
        floor = jnp.where(thr == INT_MIN, INT_MIN + 1, thr)

        def select_pair(t, carry):
            for j in (2 * t, 2 * t + 1):
                bias_ref[j] = jnp.where(keys_ref[j] >= floor, 0.0, NEG)
            return carry

        lax.fori_loop(0, npairs, select_pair, 0)

    @pl.when(i == 0)
    def _():
        def block_norm(j, best):
            kf = ka_ref[pl.ds(pl.multiple_of(j * tk, tk), tk), :].astype(F32)
            return jnp.maximum(best, jnp.max(jnp.sum(kf * kf, axis=1, keepdims=True), axis=0, keepdims=True))
        k2 = lax.fori_loop(0, ka_ref.shape[0] // tk, block_norm, jnp.zeros((1, 1), F32))
        knorm_ref[...] = jnp.broadcast_to(k2, knorm_ref.shape)

    kmax = jnp.sqrt(knorm_ref[0:1, 0:1]) * BOUND_MARGIN
    slope = kmax * (-0.5 / Q_NORM_GUESS)
    offset = kmax * (-0.5 * Q_NORM_GUESS)
    shift_lane = lax.broadcasted_iota(jnp.int32, (tq, LANES), 1) == HEAD_DIM
    plain, shifted = [], []
    for qf in split_heads(q8_ref):
        neg_bound = jnp.sum(qf * qf, axis=1, keepdims=True) * slope + offset
        plain.append(qf.astype(BF16))
        shifted.append(jnp.where(shift_lane, neg_bound, qf).astype(BF16))
    qs_plain = jnp.concatenate(plain, axis=0)
    qs = jnp.concatenate(shifted, axis=0)
    acc_ref[...] = jnp.zeros(acc_ref.shape, F32)

    last = 2 * npairs - 1

    def masked_scores(q_stack, jc):
        rows = pl.ds(pl.multiple_of(jc * tk, tk), tk)
        s = lax.dot_general(ka_ref[rows, :], q_stack, _NT, preferred_element_type=F32)
        return [s[:, h * tq:(h + 1) * tq] + bias_ref[jc] for h in range(DSA_HEADS)]

    def scores_stage(s_ref, jb):
        for h, sh in enumerate(masked_scores(qs, jnp.minimum(jb, last))):
            s_ref[:, h * tq:(h + 1) * tq] = sh

    def exp_stage(s_ref, p_ref):
        for h in range(DSA_HEADS):
            hs = slice(h * tq, (h + 1) * tq)
            p_ref[:, hs] = jnp.exp2(s_ref[:, hs]).astype(BF16)

    def value_stage(p_ref, jb):
        acc_ref[...] += jnp.dot(vet_ref[jnp.clip(jb, 0, last)], p_ref[...], preferred_element_type=F32)

    scores_stage(s0_ref, 0)
    p1_ref[...] = jnp.zeros(p1_ref.shape, BF16)

    def attend_pair(t, carry):
        j = 2 * t
        scores_stage(s1_ref, j + 1)
        exp_stage(s0_ref, p0_ref)
        value_stage(p1_ref, j - 1)
        scores_stage(s0_ref, j + 2)
        exp_stage(s1_ref, p1_ref)
        value_stage(p0_ref, j)
        return carry

    lax.fori_loop(0, npairs, attend_pair, 0)
    value_stage(p1_ref, last)

    denom = acc_ref[HEAD_DIM:HEAD_DIM + 1, :]
    underflow = jnp.max(jnp.where(denom > TINY_DENOM, 0.0, 1.0)) > 0.5

    @pl.when(underflow)
    def _():
        m_ref[...] = jnp.full(m_ref.shape, NEG, F32)
        acc_ref[...] = jnp.zeros(acc_ref.shape, F32)

        def exact_block(j, carry):
            for h, sh in enumerate(masked_scores(qs_plain, j)):
                hs = slice(h * tq, (h + 1) * tq)
                m_old = m_ref[h:h + 1, :]
                m_new = jnp.maximum(m_old, jnp.max(sh, axis=0, keepdims=True))
                alpha_ref[h:h + 1, :] = jnp.exp2(m_old - m_new)
                m_ref[h:h + 1, :] = m_new
                p0_ref[:, hs] = jnp.exp2(sh - m_new).astype(BF16)
            pv = jnp.dot(vet_ref[j], p0_ref[...], preferred_element_type=F32)
            for h in range(DSA_HEADS):
                hs = slice(h * tq, (h + 1) * tq)
                acc_ref[:, hs] = acc_ref[:, hs] * alpha_ref[h:h + 1, :] + pv[:, hs]
            return carry

        lax.fori_loop(0, 2 * npairs, exact_block, 0)

    def head_out(h):
        a = acc_ref[:, h * tq:(h + 1) * tq]
        return a[0:HEAD_DIM, :] / a[HEAD_DIM:HEAD_DIM + 1, :]

    pairs = [jnp.transpose(jnp.concatenate([head_out(2 * g), head_out(2 * g + 1)], axis=0))
             for g in range(DSA_HEADS // 2)]
    yb = jnp.concatenate(pairs, axis=1) * _silu(bg_ref[...])
    z = zp_ref[...] + jnp.dot(yb.astype(BF16), wout_ref[...], preferred_element_type=F32)
    o_ref[...] = _layer_norm(z, g_ref[...], b_ref[...])


def _dsa(q8, qi, wi, bg, zp, ka, vet, ki, w_out_b, ln_g, ln_b, *, batch, seq, tq, tk):
    n = q8.shape[0]
    nq = seq // tq
    nk = seq // tk
    topk = min(TOPK_MAX, seq // 4)
    qrow = lambda b, i: (b * nq + i, 0)
    krow = lambda b, i: (b, 0)
    qtile = lambda width: pl.BlockSpec((tq, width), qrow)
    kernel = functools.partial(_dsa_kernel, tq=tq, tk=tk, topk=float(topk))
    return pl.pallas_call(
        kernel,
        name="dsa_attention",
        grid=(batch, nq),
        in_specs=[qtile(DSA_WIDTH), qtile(IDX_HEADS * IDX_DIM), qtile(LANES), qtile(DSA_WIDTH),
                  qtile(D_MODEL),
                  pl.BlockSpec((seq, LANES), krow),
                  pl.BlockSpec((nk, VE_ROWS, tk), lambda b, i: (b, 0, 0)),
                  pl.BlockSpec((seq, LANES), krow),
                  _resident((DSA_WIDTH, D_MODEL)), _resident((1, D_MODEL)), _resident((1, D_MODEL))],
        out_specs=qtile(D_MODEL),
        out_shape=jax.ShapeDtypeStruct((n, D_MODEL), F32),
        scratch_shapes=[pltpu.VMEM((nk, tk, tq), jnp.int32),
                        pltpu.VMEM((nk, tk, tq), jnp.int16),
                        pltpu.VMEM((nk, tk, tq), jnp.int16),
                        pltpu.VMEM((nk, tk, tq), F32),
                        pltpu.VMEM((tk, DSA_HEADS * tq), F32),
                        pltpu.VMEM((tk, DSA_HEADS * tq), F32),
                        pltpu.VMEM((tk, DSA_HEADS * tq), BF16),
                        pltpu.VMEM((tk, DSA_HEADS * tq), BF16),
                        pltpu.VMEM((VE_ROWS, DSA_HEADS * tq), F32),
                        pltpu.VMEM((DSA_HEADS, tq), F32),
                        pltpu.VMEM((DSA_HEADS, tq), F32),
                        pltpu.VMEM((SUBLANES, LANES), F32)],
        compiler_params=_cparams(2),
    )(q8, qi, wi, bg, zp, ka, vet, ki, w_out_b, ln_g.reshape(1, D_MODEL), ln_b.reshape(1, D_MODEL))


def _odd_kernel(x_ref, win_ref, wout_ref, kv_ref, vg_ref, vb_ref, ws_ref, bs_ref, g_ref, b_ref, o_ref,
                vn_ref, ug_ref, y_ref, *, tt):
    x = x_ref[...]
    xb = x.astype(BF16)

    def proj(col, width):
        return jnp.dot(xb, win_ref[:, col:col + width], preferred_element_type=F32)

    vn_ref[...] = _layer_norm(_gelu_tanh(proj(SG_WIDTH, SG_WIDTH)), vg_ref[...], vb_ref[...]).astype(BF16)
    ug_ref[...] = _gelu_tanh(proj(0, SG_WIDTH)) * _silu(proj(2 * SG_WIDTH, SG_WIDTH))
    causal = (lax.broadcasted_iota(jnp.int32, (SG_CHUNK, SG_CHUNK), 1)
              <= lax.broadcasted_iota(jnp.int32, (SG_CHUNK, SG_CHUNK), 0))
    gw = SG_WIDTH // SG_GROUPS
    for g in range(SG_GROUPS):
        w = jnp.where(causal, ws_ref[g], 0.0).astype(BF16)
        bias = bs_ref[:, g:g + 1]
        cs = slice(gw * g, gw * (g + 1))
        for c in range(tt // SG_CHUNK):
            rs = slice(SG_CHUNK * c, SG_CHUNK * (c + 1))
            mixed = jnp.dot(w, vn_ref[rs, cs], preferred_element_type=F32) + bias
            y_ref[rs, cs] = (ug_ref[rs, cs] * mixed).astype(BF16)
    z = DN_ALPHA * x + jnp.dot(y_ref[...], wout_ref[0:SG_WIDTH, :], preferred_element_type=F32)
    ym = _mem_heads(proj(3 * SG_WIDTH, MEM_WIDTH), kv_ref) * _silu(proj(3 * SG_WIDTH + MEM_WIDTH, MEM_WIDTH))
    z = z + jnp.dot(ym.astype(BF16), wout_ref[SG_WIDTH:SG_WIDTH + MEM_WIDTH, :], preferred_element_type=F32)
    o_ref[...] = _layer_norm(z, g_ref[...], b_ref[...])


def _odd_layer(x, mem_kv, vln_g, vln_b, ws, bs, w_in, w_out, ln_g, ln_b, *, seq, mem_len, tt):
    n = x.shape[0]
    nt = seq // tt
    row = lambda i: (i, 0)
    vec = lambda v: v.reshape(1, -1)
    return pl.pallas_call(
        functools.partial(_odd_kernel, tt=tt),
        name="odd_layer",
        grid=(n // tt,),
        in_specs=[pl.BlockSpec((tt, D_MODEL), row),
                  _resident(w_in.shape),
                  _resident(w_out.shape),
                  pl.BlockSpec((mem_len, 2 * MEM_WIDTH), lambda i: (i // nt, 0)),
                  _resident((1, SG_WIDTH)), _resident((1, SG_WIDTH)),
                  _resident((SG_GROUPS, SG_CHUNK, SG_CHUNK)),
                  _resident((SG_CHUNK, SG_GROUPS)),
                  _resident((1, D_MODEL)), _resident((1, D_MODEL))],
        out_specs=pl.BlockSpec((tt, D_MODEL), row),
        out_shape=jax.ShapeDtypeStruct((n, D_MODEL), F32),
        scratch_shapes=[pltpu.VMEM((tt, SG_WIDTH), BF16),
                        pltpu.VMEM((tt, SG_WIDTH), F32),
                        pltpu.VMEM((tt, SG_WIDTH), BF16)],
        compiler_params=_cparams(1),
    )(x, w_in.astype(BF16), w_out.astype(BF16), mem_kv, vec(vln_g), vec(vln_b), ws, jnp.transpose(bs),
      vec(ln_g), vec(ln_b))


def _even_layer(x, mem_kv, cosf, sinf, w_in, conv_w, conv_b, cln_g, cln_b, pw2_w, pw2_b, w_out, ln_g, ln_b,
                *, batch, seq, mem_len, tt):
    tk = min(K_TILE, seq)
    zp, bg, q8, qi, wi, ka, vet, ki = _even_front(x, mem_kv, cosf, sinf, w_in, conv_w, conv_b, cln_g, cln_b,
                                                  pw2_w, pw2_b, w_out, batch=batch, seq=seq, mem_len=mem_len,
                                                  tt=tt, tk=tk)
    w_out_b = w_out[CONV_WIDTH:CONV_WIDTH + DSA_WIDTH].astype(BF16)
    return _dsa(q8, qi, wi, bg, zp, ka, vet, ki, w_out_b, ln_g, ln_b, batch=batch, seq=seq, tq=Q_TILE, tk=tk)


def kernel(x, mem, positions, e_w_in, e_conv_w, e_conv_b, e_cln_g, e_cln_b, e_pw2_w, e_pw2_b, e_w_out, o_w_in, o_vln_g, o_vln_b, o_ws, o_bs, o_w_out, mem_wk, mem_wv, ln_g, ln_b):
    batch, seq, d = x.shape
    mem_len = mem.shape[1]
    tt = min(TOKEN_TILE, seq)
    cosf, sinf = _rope_tables(positions)
    xf = x.reshape(batch * seq, d)
    memf = mem.reshape(batch * mem_len, d)
    for layer in range(DEPTH):
        j = layer // 2
        w_kv = jnp.concatenate([mem_wk[layer], mem_wv[layer]], axis=1).astype(BF16)
        mem_kv = _matmul(memf, w_kv, tm=min(tt, memf.shape[0]), tn=1024, out_dtype=BF16)
        if layer % 2 == 0:
            xf = _even_layer(xf, mem_kv, cosf, sinf, _even_w_in(e_w_in, j), e_conv_w[j], e_conv_b[j], e_cln_g[j], e_cln_b[j],
                             e_pw2_w[j], e_pw2_b[j], e_w_out[j], ln_g[layer], ln_b[layer],
                             batch=batch, seq=seq, mem_len=mem_len, tt=tt)
        else:
            xf = _odd_layer(xf, mem_kv, o_vln_g[j], o_vln_b[j], o_ws[j], o_bs[j], o_w_in[j], o_w_out[j],
                            ln_g[layer], ln_b[layer], seq=seq, mem_len=mem_len, tt=tt)
    return xf.reshape(batch, seq, d)
```

```python
import functools

import numpy as np
import jax
import jax.numpy as jnp
from jax import lax
from jax.experimental import pallas as pl
from jax.experimental.pallas import tpu as pltpu

F32 = jnp.float32
BF16 = jnp.bfloat16

D_MODEL = 1024
DEPTH = 4
HEAD_DIM = 64
CONV_WIDTH = 512
CONV_K = 31
DSA_HEADS = 8
DSA_WIDTH = 512
IDX_HEADS = 4
IDX_DIM = 64
TOPK_MAX = 256
SG_CHUNK = 128
SG_GROUPS = 8
SG_WIDTH = 1024
MEM_HEADS = 4
MEM_HEAD_DIM = 128
MEM_WIDTH = 512
ROPE_THETA = 10000.0
LN_EPS = 1e-5
DN_ALPHA = (2 * DEPTH) ** 0.25

LANES = 128
SUBLANES = 8
VE_ROWS = HEAD_DIM + 16
CONV_HALO = 32
NEG = -1e30
INT_MIN = -(2 ** 31)
HALF16 = 2 ** 15
LOW_DIGIT_SETTLE_BITS = 12
QK_SCALE_LOG2 = float(np.log2(np.e)) * HEAD_DIM ** -0.5
BOUND_MARGIN = 1.0 + 2.0 ** -8
Q_NORM_GUESS = QK_SCALE_LOG2 * HEAD_DIM ** 0.5
TINY_DENOM = 2.0 ** -100

VMEM_LIMIT = 56 * 1024 * 1024

TOKEN_TILE = 512
Q_TILE = 256
K_TILE = 256

_NT = (((1,), (1,)), ((), ()))


def _cparams(n_axes, flags=None):
    return pltpu.CompilerParams(dimension_semantics=("arbitrary",) * n_axes,
                                vmem_limit_bytes=VMEM_LIMIT, flags=flags)


def _resident(shape):
    zeros = (0,) * len(shape)
    return pl.BlockSpec(shape, lambda *_: zeros, pipeline_mode=pl.Buffered(1))


def _silu(g):
    return g * (1.0 / (1.0 + jnp.exp(-g)))


def _gelu_tanh(x):
    c = np.float32(np.sqrt(2.0 / np.pi))
    return x * (0.5 * (1.0 + jnp.tanh(c * (x + 0.044715 * (x * x * x)))))


def _layer_norm(x, g, b):
    mu = jnp.mean(x, axis=-1, keepdims=True)
    xc = x - mu
    var = jnp.mean(xc * xc, axis=-1, keepdims=True)
    return xc * lax.rsqrt(var + LN_EPS) * g + b


def _mem_heads(mq, kv_ref):
    outs = []
    for h in range(MEM_HEADS):
        hs = slice(MEM_HEAD_DIM * h, MEM_HEAD_DIM * (h + 1))
        vs = slice(MEM_WIDTH + MEM_HEAD_DIM * h, MEM_WIDTH + MEM_HEAD_DIM * (h + 1))
        s = lax.dot_general(mq[:, hs].astype(BF16), kv_ref[:, hs], _NT,
                            preferred_element_type=F32) * (MEM_HEAD_DIM ** -0.5)
        e = jnp.exp(s - jnp.max(s, axis=-1, keepdims=True))
        p = e / jnp.sum(e, axis=-1, keepdims=True)
        outs.append(jnp.dot(p.astype(BF16), kv_ref[:, vs], preferred_element_type=F32))
    return jnp.concatenate(outs, axis=1)


def _mm_kernel(x_ref, w_ref, o_ref):
    o_ref[...] = jnp.dot(x_ref[...].astype(BF16), w_ref[...],
                         preferred_element_type=F32).astype(o_ref.dtype)


def _matmul(x, w, *, tm, tn, out_dtype=F32):
    m, k = x.shape
    n = w.shape[1]
    return pl.pallas_call(
        _mm_kernel,
        name="mem_kv_proj",
        grid=(m // tm, n // tn),
        in_specs=[pl.BlockSpec((tm, k), lambda i, j: (i, 0)),
                  pl.BlockSpec((k, tn), lambda i, j: (0, j))],
        out_specs=pl.BlockSpec((tm, tn), lambda i, j: (i, j)),
        out_shape=jax.ShapeDtypeStruct((m, n), out_dtype),
        compiler_params=_cparams(2),
    )(x, w)


def _rope_table_kernel(pos_ref, inv_ref, cos_ref, sin_ref):
    ang = pos_ref[...].astype(F32) * inv_ref[...]
    half = HEAD_DIM // 2
    groups = LANES // half
    grp = lax.broadcasted_iota(jnp.int32, ang.shape, 1) // half
    sign = jnp.where((grp & 1) == 0, -1.0, 1.0)
    for table, out_ref, scale in ((jnp.cos(ang), cos_ref, None), (jnp.sin(ang), sin_ref, sign)):
        rolled = [table] + [pltpu.roll(table, half * k, 1) for k in range(1, groups)]
        for q in range(groups):
            y = rolled[(0 - q) % groups]
            for g in range(1, groups):
                y = jnp.where(grp == g, rolled[(g - q) % groups], y)
            out_ref[q] = y if scale is None else y * scale


def _rope_tables(positions):
    n = positions.size
    half = HEAD_DIM // 2
    per_row = LANES // half
    rows = n // per_row
    inv = (ROPE_THETA ** (-jnp.arange(0, HEAD_DIM, 2, dtype=F32) / HEAD_DIM)).astype(F32)
    pos_rep = jnp.repeat(jnp.transpose(positions.reshape(per_row, rows)), half, axis=1)
    inv_rep = jnp.tile(inv, per_row).reshape(1, LANES)
    tr = min(512, rows)
    cos, sin = pl.pallas_call(
        _rope_table_kernel,
        name="rope_tables",
        grid=(rows // tr,),
        in_specs=[pl.BlockSpec((tr, LANES), lambda i: (i, 0)),
                  pl.BlockSpec((1, LANES), lambda i: (0, 0))],
        out_specs=[pl.BlockSpec((per_row, tr, LANES), lambda i: (0, i, 0))] * 2,
        out_shape=[jax.ShapeDtypeStruct((per_row, rows, LANES), F32)] * 2,
        compiler_params=_cparams(1),
    )(pos_rep, inv_rep)
    return cos.reshape(n, LANES), sin.reshape(n, LANES)


_E_GATE_B = 3 * CONV_WIDTH
_E_MQ = _E_GATE_B + DSA_WIDTH
_E_MGATE = _E_MQ + MEM_WIDTH
_E_Q = _E_MGATE + MEM_WIDTH
_E_QI = _E_Q + DSA_WIDTH
_E_KKI = _E_QI + IDX_HEADS * IDX_DIM
_E_VW = _E_KKI + LANES
_E_TOTAL = _E_VW + LANES


_E_SPLIT_NAMES = ("a_val", "a_glu", "a_gate", "q", "k", "v", "qi", "ki", "wi", "b_gate", "mq", "m_gate")
_E_SPLIT_SIZES = (CONV_WIDTH, CONV_WIDTH, CONV_WIDTH, DSA_WIDTH, HEAD_DIM, HEAD_DIM,
                  IDX_HEADS * IDX_DIM, IDX_DIM, IDX_HEADS, DSA_WIDTH, MEM_WIDTH, MEM_WIDTH)
_E_SPLIT = dict(zip(_E_SPLIT_NAMES, zip(np.cumsum((0,) + _E_SPLIT_SIZES[:-1]).tolist(), _E_SPLIT_SIZES)))
_E_ORDER = ("a_val", "a_glu", "a_gate", "b_gate", "mq", "m_gate", "q", "qi", "k", "ki", "v", "wi")


def _even_w_layout_kernel(w_ref, o_ref):
    col = 0
    for name in _E_ORDER:
        src, width = _E_SPLIT[name]
        o_ref[:, col:col + width] = w_ref[:, src:src + width].astype(BF16)
        col += width
    o_ref[:, col:_E_TOTAL] = jnp.zeros((o_ref.shape[0], _E_TOTAL - col), BF16)


def _even_w_in(w_stack, layer):
    _, d, n_in = w_stack.shape
    tr = 256
    return pl.pallas_call(
        _even_w_layout_kernel,
        name="even_w_layout",
        grid=(d // tr,),
        in_specs=[pl.BlockSpec((None, tr, n_in), lambda i: (layer, i, 0))],
        out_specs=pl.BlockSpec((tr, _E_TOTAL), lambda i: (i, 0)),
        out_shape=jax.ShapeDtypeStruct((d, _E_TOTAL), BF16),
        compiler_params=_cparams(1),
    )(w_stack)


def _even_front_kernel(x_ref, win_ref, wout_ref, kv_ref, cw_ref, cb_ref, cg_ref, cbeta_ref, pw_ref, pb_ref,
                       cos_ref, sin_ref,
                       zp_ref, bg_ref, q8_ref, qio_ref, wi_ref, ka_ref, vet_ref, ki_ref,
                       hist_ref, shift_ref, *, tt, tk):
    @pl.when(pl.program_id(1) == 0)
    def _():
        hist_ref[0:CONV_HALO, :] = jnp.zeros((CONV_HALO, CONV_WIDTH), F32)

    x = x_ref[...]
    xb = x.astype(BF16)

    def proj(col, width):
        return jnp.dot(xb, win_ref[:, col:col + width], preferred_element_type=F32)

    glu = proj(CONV_WIDTH, CONV_WIDTH)
    hist_ref[CONV_HALO:CONV_HALO + tt, :] = proj(0, CONV_WIDTH) * (1.0 / (1.0 + jnp.exp(-glu)))
    span = tt + CONV_HALO - SUBLANES
    for r in range(1, SUBLANES):
        shift_ref[r - 1, 0:span, :] = hist_ref[pl.ds(r, span), :]
    base = CONV_HALO - (CONV_K - 1)
    acc = None
    for j in range(CONV_K):
        a, r = divmod(base + j, SUBLANES)
        rows = slice(SUBLANES * a, SUBLANES * a + tt)
        src = hist_ref[rows, :] if r == 0 else shift_ref[r - 1, rows, :]
        term = src * cw_ref[j:j + 1, :]
        acc = term if acc is None else acc + term
    acc = acc + cb_ref[...]
    hist_ref[0:CONV_HALO, :] = hist_ref[tt:tt + CONV_HALO, :]
    y = _silu(_layer_norm(acc, cg_ref[...], cbeta_ref[...]))
    y = jnp.dot(y.astype(BF16), pw_ref[...], preferred_element_type=F32) + pb_ref[...]
    ya = y * _silu(proj(2 * CONV_WIDTH, CONV_WIDTH))
    z = DN_ALPHA * x + jnp.dot(ya.astype(BF16), wout_ref[0:CONV_WIDTH, :], preferred_element_type=F32)

    ym = _mem_heads(proj(_E_MQ, MEM_WIDTH), kv_ref) * _silu(proj(_E_MGATE, MEM_WIDTH))
    m0 = CONV_WIDTH + DSA_WIDTH
    zp_ref[...] = z + jnp.dot(ym.astype(BF16), wout_ref[m0:m0 + MEM_WIDTH, :], preferred_element_type=F32)

    bg_ref[...] = proj(_E_GATE_B, DSA_WIDTH)
    cosf = cos_ref[...]
    sinf = sin_ref[...]

    lane = lax.broadcasted_iota(jnp.int32, (tt, LANES), 1)
    first_half = (lane & (HEAD_DIM - 1)) < HEAD_DIM // 2
    low = lane < HEAD_DIM
    marker = jnp.where(lane == HEAD_DIM, 1.0, 0.0)

    def rope(g):
        rot = jnp.where(first_half, pltpu.roll(g, LANES - HEAD_DIM // 2, 1), pltpu.roll(g, HEAD_DIM // 2, 1))
        return g * cosf + rot * sinf

    qf = proj(_E_Q, DSA_WIDTH)
    for g in range(DSA_WIDTH // LANES):
        sl = slice(LANES * g, LANES * (g + 1))
        q8_ref[:, sl] = (rope(qf[:, sl]) * QK_SCALE_LOG2).astype(BF16)
    qif = proj(_E_QI, IDX_HEADS * IDX_DIM)
    for g in range(IDX_HEADS * IDX_DIM // LANES):
        sl = slice(LANES * g, LANES * (g + 1))
        qio_ref[:, sl] = rope(qif[:, sl]).astype(BF16)
    kk = rope(proj(_E_KKI, LANES))
    ka_ref[...] = jnp.where(low, kk, marker).astype(BF16)
    ki_ref[...] = jnp.where(low, pltpu.roll(kk, HEAD_DIM, 1), 0.0).astype(BF16)
    vw = proj(_E_VW, LANES)
    ve = jnp.where(low, vw, marker)
    for c in range(tt // tk):
        vet_ref[c] = jnp.transpose(ve[tk * c:tk * (c + 1), :])[0:VE_ROWS, :].astype(BF16)
    wi_ref[...] = vw * ((IDX_HEADS ** -0.5) * (IDX_DIM ** -0.5))


def _even_front(x, mem_kv, cosf, sinf, w_in, conv_w, conv_b, cln_g, cln_b, pw_w, pw_b, w_out,
                *, batch, seq, mem_len, tt, tk):
    n = x.shape[0]
    nt = seq // tt
    wq = DSA_WIDTH
    wqi = IDX_HEADS * IDX_DIM
    cw = jnp.concatenate([conv_w, jnp.zeros((CONV_HALO - CONV_K, CONV_WIDTH), F32)], axis=0)
    vec = lambda v: v.reshape(1, -1)
    row = lambda b, i: (b * nt + i, 0)
    tile = lambda width: pl.BlockSpec((tt, width), row)
    return pl.pallas_call(
        functools.partial(_even_front_kernel, tt=tt, tk=tk),
        name="even_front",
        grid=(batch, nt),
        in_specs=[tile(D_MODEL),
                  _resident((D_MODEL, _E_TOTAL)),
                  _resident(w_out.shape),
                  pl.BlockSpec((mem_len, 2 * MEM_WIDTH), lambda b, i: (b, 0)),
                  _resident((CONV_HALO, CONV_WIDTH)),
                  _resident((1, CONV_WIDTH)), _resident((1, CONV_WIDTH)), _resident((1, CONV_WIDTH)),
                  _resident((CONV_WIDTH, CONV_WIDTH)),
                  _resident((1, CONV_WIDTH)),
                  tile(LANES), tile(LANES)],
        out_specs=[tile(D_MODEL), tile(DSA_WIDTH), tile(wq), tile(wqi), tile(LANES), tile(LANES),
                   pl.BlockSpec((tt // tk, VE_ROWS, tk), lambda b, i: (b * nt + i, 0, 0)),
                   tile(LANES)],
        out_shape=[jax.ShapeDtypeStruct((n, D_MODEL), F32), jax.ShapeDtypeStruct((n, DSA_WIDTH), F32),
                   jax.ShapeDtypeStruct((n, wq), BF16), jax.ShapeDtypeStruct((n, wqi), BF16),
                   jax.ShapeDtypeStruct((n, LANES), F32), jax.ShapeDtypeStruct((n, LANES), BF16),
                   jax.ShapeDtypeStruct((n // tk, VE_ROWS, tk), BF16), jax.ShapeDtypeStruct((n, LANES), BF16)],
        scratch_shapes=[pltpu.VMEM((CONV_HALO + tt, CONV_WIDTH), F32),
                        pltpu.VMEM((SUBLANES - 1, CONV_HALO + tt - SUBLANES, CONV_WIDTH), F32)],
        compiler_params=_cparams(2),
    )(x, w_in, w_out.astype(BF16), mem_kv, cw, vec(conv_b), vec(cln_g), vec(cln_b),
      pw_w.astype(BF16), vec(pw_b), cosf, sinf)


def _dsa_kernel(q8_ref, qi_ref, wi_ref, bg_ref, zp_ref, ka_ref, vet_ref, ki_ref, wout_ref, g_ref, b_ref, o_ref,
                keys_ref, hi_ref, lo_ref, bias_ref, s0_ref, s1_ref, p0_ref, p1_ref, acc_ref, m_ref,
                alpha_ref, knorm_ref, *, tq, tk, topk):
    i = pl.program_id(1)
    nkb = (i * tq + tq + tk - 1) // tk
    s_loc = lax.broadcasted_iota(jnp.int32, (tk, tq), 0)
    t_idx = i * tq + lax.broadcasted_iota(jnp.int32, (tk, tq), 1)

    low = lax.broadcasted_iota(jnp.int32, (tq, LANES), 1) < HEAD_DIM

    def split_heads(ref):
        heads = []
        for g in range(ref.shape[1] // LANES):
            two = ref[:, LANES * g:LANES * (g + 1)].astype(F32)
            heads += [jnp.where(low, two, 0.0), jnp.where(low, pltpu.roll(two, HEAD_DIM, 1), 0.0)]
        return heads

    qis = jnp.concatenate([qh.astype(BF16) for qh in split_heads(qi_ref)], axis=0)
    wt = jnp.transpose(wi_ref[...])[HEAD_DIM:HEAD_DIM + SUBLANES, :]

    npairs = (nkb + 1) // 2

    def score_block(j):
        kib = ki_ref[pl.ds(pl.multiple_of(j * tk, tk), tk), :]
        lg = lax.dot_general(kib, qis, _NT, preferred_element_type=F32)
        sc = jnp.maximum(lg[:, 0:tq], 0.0) * wt[0:1, :]
        for h in range(1, IDX_HEADS):
            sc = sc + jnp.maximum(lg[:, h * tq:(h + 1) * tq], 0.0) * wt[h:h + 1, :]
        bits = pltpu.bitcast(sc, jnp.int32)
        key = bits ^ ((bits >> 31) & 0x7FFFFFFF)
        key = jnp.where(j * tk + s_loc <= t_idx, key, INT_MIN)
        keys_ref[j] = key
        hi_ref[j] = (key >> 16).astype(jnp.int16)
        lo_ref[j] = ((key & 0xFFFF) - HALF16).astype(jnp.int16)

    def score_pair(t, carry):
        score_block(2 * t)
        score_block(2 * t + 1)
        return carry

    lax.fori_loop(0, npairs, score_pair, 0)

    nacc = tk // 4
    one, zero = jnp.asarray(1.0, BF16), jnp.asarray(0.0, BF16)

    def count16(d_ref, pred_fn):
        def chunks(j):
            r = jnp.where(pred_fn(d_ref[j]), one, zero).reshape(4, nacc, tq)
            return (r[0] + r[1]) + (r[2] + r[3])

        def pair(t, acc):
            return acc + (chunks(2 * t) + chunks(2 * t + 1))
        acc = lax.fori_loop(0, npairs, pair, jnp.zeros((nacc, tq), BF16))
        return jnp.sum(acc.astype(F32), axis=0, keepdims=True)

    def digit_search(d_ref, target, settle_after=None):
        def bit_step(b, carry):
            prefix, reach = carry
            cand = prefix | lax.shift_left(jnp.int32(1), 15 - b)
            cand_s = (cand - HALF16).astype(jnp.int16)
            cnt = count16(d_ref, lambda d: d >= cand_s)
            take = cnt >= target
            return jnp.where(take, cand, prefix), jnp.where(take, cnt, reach)

        carry = (jnp.zeros((1, tq), jnp.int32), jnp.full((1, tq), 4.0 * tk * keys_ref.shape[0], F32))
        if settle_after is None:
            return lax.fori_loop(0, 16, bit_step, carry)[0]
        carry = lax.fori_loop(0, settle_after, bit_step, carry)
        settled = jnp.min(jnp.where(carry[1] == target, 1.0, 0.0)) > 0.5
        return lax.cond(settled, lambda c: c, lambda c: lax.fori_loop(settle_after, 16, bit_step, c), carry)[0]

    thr_hi = digit_search(hi_ref, topk) - HALF16
    thr_hi16 = thr_hi.astype(jnp.int16)
    c_hi = count16(hi_ref, lambda d: d > thr_hi16)

    def low_digit_pair(t, carry):
        for j in (2 * t, 2 * t + 1):
            lo_ref[j] = jnp.where(hi_ref[j] == thr_hi16, lo_ref[j], jnp.asarray(-HALF16, jnp.int16))
        return carry

    lax.fori_loop(0, npairs, low_digit_pair, 0)
    thr_lo = digit_search(lo_ref, topk - c_hi, settle_after=LOW_DIGIT_SETTLE_BITS)
    thr_lo16 = (thr_lo - HALF16).astype(jnp.int16)
    c_gt = c_hi + count16(lo_ref, lambda d: d > thr_lo16)
    thr = lax.shift_left(thr_hi, 16) | thr_lo
    n_tie = jnp.where(thr == INT_MIN, 0.0, topk - c_gt)
    c_eq = count16(lo_ref, lambda d: d == thr_lo16)
    excess = jnp.max(jnp.where(thr == INT_MIN, 0.0, c_eq - n_tie))
    some_excess = excess > 0.5

    @pl.when(some_excess)
    def _():
        lower = jnp.where(lax.broadcasted_iota(jnp.int32, (tk, tk), 1)
                          < lax.broadcasted_iota(jnp.int32, (tk, tk), 0), 1.0, 0.0).astype(BF16)

        def select_block(j, off):
            kb = keys_ref[j]
            tie = kb == thr
            tie_f = jnp.where(tie, 1.0, 0.0)
            before = jnp.dot(lower, tie_f.astype(BF16), preferred_element_type=F32) + off
            bias_ref[j] = jnp.where(kb > thr, 0.0,
                                    jnp.where(tie, jnp.where(before < n_tie, 0.0, NEG), NEG))
            return off + jnp.sum(tie_f, axis=0, keepdims=True)

        def select_pair(t, off):
            return select_block(2 * t + 1, select_block(2 * t, off))

        lax.fori_loop(0, npairs, select_pair, jnp.zeros((1, tq), F32))

    @pl.when(jnp.logical_not(some_excess))
    def _():
        floor = jnp.where(thr == INT_MIN, INT_MIN + 1, thr)

        def select_pair(t, carry):
            for j in (2 * t, 2 * t + 1):
                bias_ref[j] = jnp.where(keys_ref[j] >= floor, 0.0, NEG)
            return carry

        lax.fori_loop(0, npairs, select_pair, 0)

    @pl.when(i == 0)
    def _():
        def block_norm(j, best):
            kf = ka_ref[pl.ds(pl.multiple_of(j * tk, tk), tk), :].astype(F32)
            return jnp.maximum(best, jnp.max(jnp.sum(kf * kf, axis=1, keepdims=True), axis=0, keepdims=True))
        k2 = lax.fori_loop(0, ka_ref.shape[0] // tk, block_norm, jnp.zeros((1, 1), F32))
        knorm_ref[...] = jnp.broadcast_to(k2, knorm_ref.shape)

    kmax = jnp.sqrt(knorm_ref[0:1, 0:1]) * BOUND_MARGIN
    slope = kmax * (-0.5 / Q_NORM_GUESS)
    offset = kmax * (-0.5 * Q_NORM_GUESS)
    shift_lane = lax.broadcasted_iota(jnp.int32, (tq, LANES), 1) == HEAD_DIM
    plain, shifted = [], []
    for qf in split_heads(q8_ref):
        neg_bound = jnp.sum(qf * qf, axis=1, keepdims=True) * slope + offset
        plain.append(qf.astype(BF16))
        shifted.append(jnp.where(shift_lane, neg_bound, qf).astype(BF16))
    qs_plain = jnp.concatenate(plain, axis=0)
    qs = jnp.concatenate(shifted, axis=0)
    acc_ref[...] = jnp.zeros(acc_ref.shape, F32)

    last = 2 * npairs - 1

    def masked_scores(q_stack, jc):
        rows = pl.ds(pl.multiple_of(jc * tk, tk), tk)
        s = lax.dot_general(ka_ref[rows, :], q_stack, _NT, preferred_element_type=F32)
        return [s[:, h * tq:(h + 1) * tq] + bias_ref[jc] for h in range(DSA_HEADS)]

    def scores_stage(s_ref, jb):
        for h, sh in enumerate(masked_scores(qs, jnp.minimum(jb, last))):
            s_ref[:, h * tq:(h + 1) * tq] = sh

    def exp_stage(s_ref, p_ref):
        for h in range(DSA_HEADS):
            hs = slice(h * tq, (h + 1) * tq)
            p_ref[:, hs] = jnp.exp2(s_ref[:, hs]).astype(BF16)

    def value_stage(p_ref, jb):
        acc_ref[...] += jnp.dot(vet_ref[jnp.clip(jb, 0, last)], p_ref[...], preferred_element_type=F32)

    scores_stage(s0_ref, 0)
    p1_ref[...] = jnp.zeros(p1_ref.shape, BF16)

    def attend_pair(t, carry):
        j = 2 * t
        scores_stage(s1_ref, j + 1)
        exp_stage(s0_ref, p0_ref)
        value_stage(p1_ref, j - 1)
        scores_stage(s0_ref, j + 2)
        exp_stage(s1_ref, p1_ref)
        value_stage(p0_ref, j)
        return carry

    lax.fori_loop(0, npairs, attend_pair, 0)
    value_stage(p1_ref, last)

    denom = acc_ref[HEAD_DIM:HEAD_DIM + 1, :]
    underflow = jnp.max(jnp.where(denom > TINY_DENOM, 0.0, 1.0)) > 0.5

    @pl.when(underflow)
    def _():
        m_ref[...] = jnp.full(m_ref.shape, NEG, F32)
        acc_ref[...] = jnp.zeros(acc_ref.shape, F32)

        def exact_block(j, carry):
            for h, sh in enumerate(masked_scores(qs_plain, j)):
                hs = slice(h * tq, (h + 1) * tq)
                m_old = m_ref[h:h + 1, :]
                m_new = jnp.maximum(m_old, jnp.max(sh, axis=0, keepdims=True))
                alpha_ref[h:h + 1, :] = jnp.exp2(m_old - m_new)
                m_ref[h:h + 1, :] = m_new
                p0_ref[:, hs] = jnp.exp2(sh - m_new).astype(BF16)
            pv = jnp.dot(vet_ref[j], p0_ref[...], preferred_element_type=F32)
            for h in range(DSA_HEADS):
                hs = slice(h * tq, (h + 1) * tq)
                acc_ref[:, hs] = acc_ref[:, hs] * alpha_ref[h:h + 1, :] + pv[:, hs]
            return carry

        lax.fori_loop(0, 2 * npairs, exact_block, 0)

    def head_out(h):
        a = acc_ref[:, h * tq:(h + 1) * tq]
        return a[0:HEAD_DIM, :] / a[HEAD_DIM:HEAD_DIM + 1, :]

    pairs = [jnp.transpose(jnp.concatenate([head_out(2 * g), head_out(2 * g + 1)], axis=0))
             for g in range(DSA_HEADS // 2)]
    yb = jnp.concatenate(pairs, axis=1) * _silu(bg_ref[...])
    z = zp_ref[...] + jnp.dot(yb.astype(BF16), wout_ref[...], preferred_element_type=F32)
    o_ref[...] = _layer_norm(z, g_ref[...], b_ref[...])


def _dsa(q8, qi, wi, bg, zp, ka, vet, ki, w_out_b, ln_g, ln_b, *, batch, seq, tq, tk):
    n = q8.shape[0]
    nq = seq // tq
    nk = seq // tk
    topk = min(TOPK_MAX, seq // 4)
    qrow = lambda b, i: (b * nq + i, 0)
    krow = lambda b, i: (b, 0)
    qtile = lambda width: pl.BlockSpec((tq, width), qrow)
    kernel = functools.partial(_dsa_kernel, tq=tq, tk=tk, topk=float(topk))
    return pl.pallas_call(
        kernel,
        name="dsa_attention",
        grid=(batch, nq),
        in_specs=[qtile(DSA_WIDTH), qtile(IDX_HEADS * IDX_DIM), qtile(LANES), qtile(DSA_WIDTH),
                  qtile(D_MODEL),
                  pl.BlockSpec((seq, LANES), krow),
                  pl.BlockSpec((nk, VE_ROWS, tk), lambda b, i: (b, 0, 0)),
                  pl.BlockSpec((seq, LANES), krow),
                  _resident((DSA_WIDTH, D_MODEL)), _resident((1, D_MODEL)), _resident((1, D_MODEL))],
        out_specs=qtile(D_MODEL),
        out_shape=jax.ShapeDtypeStruct((n, D_MODEL), F32),
        scratch_shapes=[pltpu.VMEM((nk, tk, tq), jnp.int32),
                        pltpu.VMEM((nk, tk, tq), jnp.int16),
                        pltpu.VMEM((nk, tk, tq), jnp.int16),
                        pltpu.VMEM((nk, tk, tq), F32),
                        pltpu.VMEM((tk, DSA_HEADS * tq), F32),
                        pltpu.VMEM((tk, DSA_HEADS * tq), F32),
                        pltpu.VMEM((tk, DSA_HEADS * tq), BF16),
                        pltpu.VMEM((tk, DSA_HEADS * tq), BF16),
                        pltpu.VMEM((VE_ROWS, DSA_HEADS * tq), F32),
                        pltpu.VMEM((DSA_HEADS, tq), F32),
                        pltpu.VMEM((DSA_HEADS, tq), F32),
                        pltpu.VMEM((SUBLANES, LANES), F32)],
        compiler_params=_cparams(2),
    )(q8, qi, wi, bg, zp, ka, vet, ki, w_out_b, ln_g.reshape(1, D_MODEL), ln_b.reshape(1, D_MODEL))


def _odd_kernel(x_ref, win_ref, wout_ref, kv_ref, vg_ref, vb_ref, ws_ref, bs_ref, g_ref, b_ref, o_ref,
                vn_ref, ug_ref, y_ref, *, tt):
    x = x_ref[...]
    xb = x.astype(BF16)

    def proj(col, width):
        return jnp.dot(xb, win_ref[:, col:col + width], preferred_element_type=F32)

    vn_ref[...] = _layer_norm(_gelu_tanh(proj(SG_WIDTH, SG_WIDTH)), vg_ref[...], vb_ref[...]).astype(BF16)
    ug_ref[...] = _gelu_tanh(proj(0, SG_WIDTH)) * _silu(proj(2 * SG_WIDTH, SG_WIDTH))
    causal = (lax.broadcasted_iota(jnp.int32, (SG_CHUNK, SG_CHUNK), 1)
              <= lax.broadcasted_iota(jnp.int32, (SG_CHUNK, SG_CHUNK), 0))
    gw = SG_WIDTH // SG_GROUPS
    for g in range(SG_GROUPS):
        w = jnp.where(causal, ws_ref[g], 0.0).astype(BF16)
        bias = bs_ref[:, g:g + 1]
        cs = slice(gw * g, gw * (g + 1))
        for c in range(tt // SG_CHUNK):
            rs = slice(SG_CHUNK * c, SG_CHUNK * (c + 1))
            mixed = jnp.dot(w, vn_ref[rs, cs], preferred_element_type=F32) + bias
            y_ref[rs, cs] = (ug_ref[rs, cs] * mixed).astype(BF16)
    z = DN_ALPHA * x + jnp.dot(y_ref[...], wout_ref[0:SG_WIDTH, :], preferred_element_type=F32)
    ym = _mem_heads(proj(3 * SG_WIDTH, MEM_WIDTH), kv_ref) * _silu(proj(3 * SG_WIDTH + MEM_WIDTH, MEM_WIDTH))
    z = z + jnp.dot(ym.astype(BF16), wout_ref[SG_WIDTH:SG_WIDTH + MEM_WIDTH, :], preferred_element_type=F32)
    o_ref[...] = _layer_norm(z, g_ref[...], b_ref[...])


def _odd_layer(x, mem_kv, vln_g, vln_b, ws, bs, w_in, w_out, ln_g, ln_b, *, seq, mem_len, tt):
    n = x.shape[0]
    nt = seq // tt
    row = lambda i: (i, 0)
    vec = lambda v: v.reshape(1, -1)
    return pl.pallas_call(
        functools.partial(_odd_kernel, tt=tt),
        name="odd_layer",
        grid=(n // tt,),
        in_specs=[pl.BlockSpec((tt, D_MODEL), row),
                  _resident(w_in.shape),
                  _resident(w_out.shape),
                  pl.BlockSpec((mem_len, 2 * MEM_WIDTH), lambda i: (i // nt, 0)),
                  _resident((1, SG_WIDTH)), _resident((1, SG_WIDTH)),
                  _resident((SG_GROUPS, SG_CHUNK, SG_CHUNK)),
                  _resident((SG_CHUNK, SG_GROUPS)),
                  _resident((1, D_MODEL)), _resident((1, D_MODEL))],
        out_specs=pl.BlockSpec((tt, D_MODEL), row),
        out_shape=jax.ShapeDtypeStruct((n, D_MODEL), F32),
        scratch_shapes=[pltpu.VMEM((tt, SG_WIDTH), BF16),
                        pltpu.VMEM((tt, SG_WIDTH), F32),
                        pltpu.VMEM((tt, SG_WIDTH), BF16)],
        compiler_params=_cparams(1),
    )(x, w_in.astype(BF16), w_out.astype(BF16), mem_kv, vec(vln_g), vec(vln_b), ws, jnp.transpose(bs),
      vec(ln_g), vec(ln_b))


def _even_layer(x, mem_kv, cosf, sinf, w_in, conv_w, conv_b, cln_g, cln_b, pw2_w, pw2_b, w_out, ln_g, ln_b,
                *, batch, seq, mem_len, tt):
    tk = min(K_TILE, seq)
    zp, bg, q8, qi, wi, ka, vet, ki = _even_front(x, mem_kv, cosf, sinf, w_in, conv_w, conv_b, cln_g, cln_b,
                                                  pw2_w, pw2_b, w_out, batch=batch, seq=seq, mem_len=mem_len,
                                                  tt=tt, tk=tk)
    w_out_b = w_out[CONV_WIDTH:CONV_WIDTH + DSA_WIDTH].astype(BF16)
    return _dsa(q8, qi, wi, bg, zp, ka, vet, ki, w_out_b, ln_g, ln_b, batch=batch, seq=seq, tq=Q_TILE, tk=tk)


def kernel(x, mem, positions, e_w_in, e_conv_w, e_conv_b, e_cln_g, e_cln_b, e_pw2_w, e_pw2_b, e_w_out, o_w_in, o_vln_g, o_vln_b, o_ws, o_bs, o_w_out, mem_wk, mem_wv, ln_g, ln_b):
    batch, seq, d = x.shape
    mem_len = mem.shape[1]
    tt = min(TOKEN_TILE, seq)
    cosf, sinf = _rope_tables(positions)
    xf = x.reshape(batch * seq, d)
    memf = mem.reshape(batch * mem_len, d)
    for layer in range(DEPTH):
        j = layer // 2
        w_kv = jnp.concatenate([mem_wk[layer], mem_wv[layer]], axis=1).astype(BF16)
        mem_kv = _matmul(memf, w_kv, tm=min(tt, memf.shape[0]), tn=1024, out_dtype=BF16)
        if layer % 2 == 0:
            xf = _even_layer(xf, mem_kv, cosf, sinf, _even_w_in(e_w_in, j), e_conv_w[j], e_conv_b[j], e_cln_g[j], e_cln_b[j],
                             e_pw2_w[j], e_pw2_b[j], e_w_out[j], ln_g[layer], ln_b[layer],
                             batch=batch, seq=seq, mem_len=mem_len, tt=tt)
        else:
            xf = _odd_layer(xf, mem_kv, o_vln_g[j], o_vln_b[j], o_ws[j], o_bs[j], o_w_in[j], o_w_out[j],
                            ln_g[layer], ln_b[layer], seq=seq, mem_len=mem_len, tt=tt)
    return xf.reshape(batch, seq, d)
```

```python
import functools

import numpy as np
import jax
import jax.numpy as jnp
from jax import lax
from jax.experimental import pallas as pl
from jax.experimental.pallas import tpu as pltpu

F32 = jnp.float32
BF16 = jnp.bfloat16

D_MODEL = 1024
DEPTH = 4
HEAD_DIM = 64
CONV_WIDTH = 512
CONV_K = 31
DSA_HEADS = 8
DSA_WIDTH = 512
IDX_HEADS = 4
IDX_DIM = 64
TOPK_MAX = 256
SG_CHUNK = 128
SG_GROUPS = 8
SG_WIDTH = 1024
MEM_HEADS = 4
MEM_HEAD_DIM = 128
MEM_WIDTH = 512
ROPE_THETA = 10000.0
LN_EPS = 1e-5
DN_ALPHA = (2 * DEPTH) ** 0.25

LANES = 128
SUBLANES = 8
VE_ROWS = HEAD_DIM + 16
CONV_HALO = 32
NEG = -1e30
INT_MIN = -(2 ** 31)
HALF16 = 2 ** 15
QK_SCALE_LOG2 = float(np.log2(np.e)) * HEAD_DIM ** -0.5
BOUND_MARGIN = 1.0 + 2.0 ** -8
Q_NORM_GUESS = QK_SCALE_LOG2 * HEAD_DIM ** 0.5
TINY_DENOM = 2.0 ** -100

VMEM_LIMIT = 56 * 1024 * 1024

TOKEN_TILE = 512
Q_TILE = 256
K_TILE = 512

_NT = (((1,), (1,)), ((), ()))


def _cparams(n_axes, flags=None):
    return pltpu.CompilerParams(dimension_semantics=("arbitrary",) * n_axes,
                                vmem_limit_bytes=VMEM_LIMIT, flags=flags)


def _resident(shape):
    zeros = (0,) * len(shape)
    return pl.BlockSpec(shape, lambda *_: zeros, pipeline_mode=pl.Buffered(1))


def _silu(g):
    return g * (1.0 / (1.0 + jnp.exp(-g)))


def _gelu_tanh(x):
    c = np.float32(np.sqrt(2.0 / np.pi))
    return x * (0.5 * (1.0 + jnp.tanh(c * (x + 0.044715 * (x * x * x)))))


def _layer_norm(x, g, b):
    mu = jnp.mean(x, axis=-1, keepdims=True)
    xc = x - mu
    var = jnp.mean(xc * xc, axis=-1, keepdims=True)
    return xc * lax.rsqrt(var + LN_EPS) * g + b


def _mem_heads(mq, kv_ref):
    outs = []
    for h in range(MEM_HEADS):
        hs = slice(MEM_HEAD_DIM * h, MEM_HEAD_DIM * (h + 1))
        vs = slice(MEM_WIDTH + MEM_HEAD_DIM * h, MEM_WIDTH + MEM_HEAD_DIM * (h + 1))
        s = lax.dot_general(mq[:, hs].astype(BF16), kv_ref[:, hs], _NT,
                            preferred_element_type=F32) * (MEM_HEAD_DIM ** -0.5)
        e = jnp.exp(s - jnp.max(s, axis=-1, keepdims=True))
        p = e / jnp.sum(e, axis=-1, keepdims=True)
        outs.append(jnp.dot(p.astype(BF16), kv_ref[:, vs], preferred_element_type=F32))
    return jnp.concatenate(outs, axis=1)


def _mm_kernel(x_ref, w_ref, o_ref):
    o_ref[...] = jnp.dot(x_ref[...].astype(BF16), w_ref[...],
                         preferred_element_type=F32).astype(o_ref.dtype)


def _matmul(x, w, *, tm, tn, out_dtype=F32):
    m, k = x.shape
    n = w.shape[1]
    return pl.pallas_call(
        _mm_kernel,
        name="mem_kv_proj",
        grid=(m // tm, n // tn),
        in_specs=[pl.BlockSpec((tm, k), lambda i, j: (i, 0)),
                  pl.BlockSpec((k, tn), lambda i, j: (0, j))],
        out_specs=pl.BlockSpec((tm, tn), lambda i, j: (i, j)),
        out_shape=jax.ShapeDtypeStruct((m, n), out_dtype),
        compiler_params=_cparams(2),
    )(x, w)


def _rope_table_kernel(pos_ref, inv_ref, cos_ref, sin_ref):
    ang = pos_ref[...].astype(F32) * inv_ref[...]
    half = HEAD_DIM // 2
    groups = LANES // half
    grp = lax.broadcasted_iota(jnp.int32, ang.shape, 1) // half
    sign = jnp.where((grp & 1) == 0, -1.0, 1.0)
    for table, out_ref, scale in ((jnp.cos(ang), cos_ref, None), (jnp.sin(ang), sin_ref, sign)):
        rolled = [table] + [pltpu.roll(table, half * k, 1) for k in range(1, groups)]
        for q in range(groups):
            y = rolled[(0 - q) % groups]
            for g in range(1, groups):
                y = jnp.where(grp == g, rolled[(g - q) % groups], y)
            out_ref[q] = y if scale is None else y * scale


def _rope_tables(positions):
    n = positions.size
    half = HEAD_DIM // 2
    per_row = LANES // half
    rows = n // per_row
    inv = (ROPE_THETA ** (-jnp.arange(0, HEAD_DIM, 2, dtype=F32) / HEAD_DIM)).astype(F32)
    pos_rep = jnp.repeat(jnp.transpose(positions.reshape(per_row, rows)), half, axis=1)
    inv_rep = jnp.tile(inv, per_row).reshape(1, LANES)
    tr = min(512, rows)
    cos, sin = pl.pallas_call(
        _rope_table_kernel,
        name="rope_tables",
        grid=(rows // tr,),
        in_specs=[pl.BlockSpec((tr, LANES), lambda i: (i, 0)),
                  pl.BlockSpec((1, LANES), lambda i: (0, 0))],
        out_specs=[pl.BlockSpec((per_row, tr, LANES), lambda i: (0, i, 0))] * 2,
        out_shape=[jax.ShapeDtypeStruct((per_row, rows, LANES), F32)] * 2,
        compiler_params=_cparams(1),
    )(pos_rep, inv_rep)
    return cos.reshape(n, LANES), sin.reshape(n, LANES)


_E_GATE_B = 3 * CONV_WIDTH
_E_MQ = _E_GATE_B + DSA_WIDTH
_E_MGATE = _E_MQ + MEM_WIDTH
_E_Q = _E_MGATE + MEM_WIDTH
_E_QI = _E_Q + DSA_WIDTH
_E_KKI = _E_QI + IDX_HEADS * IDX_DIM
_E_VW = _E_KKI + LANES
_E_TOTAL = _E_VW + LANES


_E_SPLIT_NAMES = ("a_val", "a_glu", "a_gate", "q", "k", "v", "qi", "ki", "wi", "b_gate", "mq", "m_gate")
_E_SPLIT_SIZES = (CONV_WIDTH, CONV_WIDTH, CONV_WIDTH, DSA_WIDTH, HEAD_DIM, HEAD_DIM,
                  IDX_HEADS * IDX_DIM, IDX_DIM, IDX_HEADS, DSA_WIDTH, MEM_WIDTH, MEM_WIDTH)
_E_SPLIT = dict(zip(_E_SPLIT_NAMES, zip(np.cumsum((0,) + _E_SPLIT_SIZES[:-1]).tolist(), _E_SPLIT_SIZES)))
_E_ORDER = ("a_val", "a_glu", "a_gate", "b_gate", "mq", "m_gate", "q", "qi", "k", "ki", "v", "wi")


def _even_w_layout_kernel(w_ref, o_ref):
    col = 0
    for name in _E_ORDER:
        src, width = _E_SPLIT[name]
        o_ref[:, col:col + width] = w_ref[:, src:src + width].astype(BF16)
        col += width
    o_ref[:, col:_E_TOTAL] = jnp.zeros((o_ref.shape[0], _E_TOTAL - col), BF16)


def _even_w_in(w_stack, layer):
    _, d, n_in = w_stack.shape
    tr = 256
    return pl.pallas_call(
        _even_w_layout_kernel,
        name="even_w_layout",
        grid=(d // tr,),
        in_specs=[pl.BlockSpec((None, tr, n_in), lambda i: (layer, i, 0))],
        out_specs=pl.BlockSpec((tr, _E_TOTAL), lambda i: (i, 0)),
        out_shape=jax.ShapeDtypeStruct((d, _E_TOTAL), BF16),
        compiler_params=_cparams(1),
    )(w_stack)


def _even_front_kernel(x_ref, win_ref, wout_ref, kv_ref, cw_ref, cb_ref, cg_ref, cbeta_ref, pw_ref, pb_ref,
                       cos_ref, sin_ref,
                       zp_ref, bg_ref, q8_ref, qio_ref, wi_ref, ka_ref, vet_ref, ki_ref,
                       hist_ref, shift_ref, *, tt, tk):
    @pl.when(pl.program_id(1) == 0)
    def _():
        hist_ref[0:CONV_HALO, :] = jnp.zeros((CONV_HALO, CONV_WIDTH), F32)

    x = x_ref[...]
    xb = x.astype(BF16)

    def proj(col, width):
        return jnp.dot(xb, win_ref[:, col:col + width], preferred_element_type=F32)

    glu = proj(CONV_WIDTH, CONV_WIDTH)
    hist_ref[CONV_HALO:CONV_HALO + tt, :] = proj(0, CONV_WIDTH) * (1.0 / (1.0 + jnp.exp(-glu)))
    span = tt + CONV_HALO - SUBLANES
    for r in range(1, SUBLANES):
        shift_ref[r - 1, 0:span, :] = hist_ref[pl.ds(r, span), :]
    base = CONV_HALO - (CONV_K - 1)
    acc = None
    for j in range(CONV_K):
        a, r = divmod(base + j, SUBLANES)
        rows = slice(SUBLANES * a, SUBLANES * a + tt)
        src = hist_ref[rows, :] if r == 0 else shift_ref[r - 1, rows, :]
        term = src * cw_ref[j:j + 1, :]
        acc = term if acc is None else acc + term
    acc = acc + cb_ref[...]
    hist_ref[0:CONV_HALO, :] = hist_ref[tt:tt + CONV_HALO, :]
    y = _silu(_layer_norm(acc, cg_ref[...], cbeta_ref[...]))
    y = jnp.dot(y.astype(BF16), pw_ref[...], preferred_element_type=F32) + pb_ref[...]
    ya = y * _silu(proj(2 * CONV_WIDTH, CONV_WIDTH))
    z = DN_ALPHA * x + jnp.dot(ya.astype(BF16), wout_ref[0:CONV_WIDTH, :], preferred_element_type=F32)

    ym = _mem_heads(proj(_E_MQ, MEM_WIDTH), kv_ref) * _silu(proj(_E_MGATE, MEM_WIDTH))
    m0 = CONV_WIDTH + DSA_WIDTH
    zp_ref[...] = z + jnp.dot(ym.astype(BF16), wout_ref[m0:m0 + MEM_WIDTH, :], preferred_element_type=F32)

    bg_ref[...] = proj(_E_GATE_B, DSA_WIDTH)
    cosf = cos_ref[...]
    sinf = sin_ref[...]

    lane = lax.broadcasted_iota(jnp.int32, (tt, LANES), 1)
    first_half = (lane & (HEAD_DIM - 1)) < HEAD_DIM // 2
    low = lane < HEAD_DIM
    marker = jnp.where(lane == HEAD_DIM, 1.0, 0.0)

    def rope(g):
        rot = jnp.where(first_half, pltpu.roll(g, LANES - HEAD_DIM // 2, 1), pltpu.roll(g, HEAD_DIM // 2, 1))
        return g * cosf + rot * sinf

    qf = proj(_E_Q, DSA_WIDTH)
    for g in range(DSA_WIDTH // LANES):
        sl = slice(LANES * g, LANES * (g + 1))
        q8_ref[:, sl] = (rope(qf[:, sl]) * QK_SCALE_LOG2).astype(BF16)
    qif = proj(_E_QI, IDX_HEADS * IDX_DIM)
    for g in range(IDX_HEADS * IDX_DIM // LANES):
        sl = slice(LANES * g, LANES * (g + 1))
        qio_ref[:, sl] = rope(qif[:, sl]).astype(BF16)
    kk = rope(proj(_E_KKI, LANES))
    ka_ref[...] = jnp.where(low, kk, marker).astype(BF16)
    ki_ref[...] = jnp.where(low, pltpu.roll(kk, HEAD_DIM, 1), 0.0).astype(BF16)
    vw = proj(_E_VW, LANES)
    ve = jnp.where(low, vw, marker)
    for c in range(tt // tk):
        vet_ref[c] = jnp.transpose(ve[tk * c:tk * (c + 1), :])[0:VE_ROWS, :].astype(BF16)
    wi_ref[...] = vw * ((IDX_HEADS ** -0.5) * (IDX_DIM ** -0.5))


def _even_front(x, mem_kv, cosf, sinf, w_in, conv_w, conv_b, cln_g, cln_b, pw_w, pw_b, w_out,
                *, batch, seq, mem_len, tt, tk):
    n = x.shape[0]
    nt = seq // tt
    wq = DSA_WIDTH
    wqi = IDX_HEADS * IDX_DIM
    cw = jnp.concatenate([conv_w, jnp.zeros((CONV_HALO - CONV_K, CONV_WIDTH), F32)], axis=0)
    vec = lambda v: v.reshape(1, -1)
    row = lambda b, i: (b * nt + i, 0)
    tile = lambda width: pl.BlockSpec((tt, width), row)
    return pl.pallas_call(
        functools.partial(_even_front_kernel, tt=tt, tk=tk),
        name="even_front",
        grid=(batch, nt),
        in_specs=[tile(D_MODEL),
                  _resident((D_MODEL, _E_TOTAL)),
                  _resident(w_out.shape),
                  pl.BlockSpec((mem_len, 2 * MEM_WIDTH), lambda b, i: (b, 0)),
                  _resident((CONV_HALO, CONV_WIDTH)),
                  _resident((1, CONV_WIDTH)), _resident((1, CONV_WIDTH)), _resident((1, CONV_WIDTH)),
                  _resident((CONV_WIDTH, CONV_WIDTH)),
                  _resident((1, CONV_WIDTH)),
                  tile(LANES), tile(LANES)],
        out_specs=[tile(D_MODEL), tile(DSA_WIDTH), tile(wq), tile(wqi), tile(LANES), tile(LANES),
                   pl.BlockSpec((tt // tk, VE_ROWS, tk), lambda b, i: (b * nt + i, 0, 0)),
                   tile(LANES)],
        out_shape=[jax.ShapeDtypeStruct((n, D_MODEL), F32), jax.ShapeDtypeStruct((n, DSA_WIDTH), F32),
                   jax.ShapeDtypeStruct((n, wq), BF16), jax.ShapeDtypeStruct((n, wqi), BF16),
                   jax.ShapeDtypeStruct((n, LANES), F32), jax.ShapeDtypeStruct((n, LANES), BF16),
                   jax.ShapeDtypeStruct((n // tk, VE_ROWS, tk), BF16), jax.ShapeDtypeStruct((n, LANES), BF16)],
        scratch_shapes=[pltpu.VMEM((CONV_HALO + tt, CONV_WIDTH), F32),
                        pltpu.VMEM((SUBLANES - 1, CONV_HALO + tt - SUBLANES, CONV_WIDTH), F32)],
        compiler_params=_cparams(2),
    )(x, w_in, w_out.astype(BF16), mem_kv, cw, vec(conv_b), vec(cln_g), vec(cln_b),
      pw_w.astype(BF16), vec(pw_b), cosf, sinf)


def _dsa_kernel(q8_ref, qi_ref, wi_ref, bg_ref, zp_ref, ka_ref, vet_ref, ki_ref, wout_ref, g_ref, b_ref, o_ref,
                keys_ref, hi_ref, lo_ref, bias_ref, s0_ref, s1_ref, p0_ref, p1_ref, acc_ref, m_ref,
                alpha_ref, knorm_ref, *, tq, tk, topk):
    i = pl.program_id(1)
    nkb = (i * tq + tq + tk - 1) // tk
    s_loc = lax.broadcasted_iota(jnp.int32, (tk, tq), 0)
    t_idx = i * tq + lax.broadcasted_iota(jnp.int32, (tk, tq), 1)

    low = lax.broadcasted_iota(jnp.int32, (tq, LANES), 1) < HEAD_DIM

    def split_heads(ref):
        heads = []
        for g in range(ref.shape[1] // LANES):
            two = ref[:, LANES * g:LANES * (g + 1)].astype(F32)
            heads += [jnp.where(low, two, 0.0), jnp.where(low, pltpu.roll(two, HEAD_DIM, 1), 0.0)]
        return heads

    qis = jnp.concatenate([qh.astype(BF16) for qh in split_heads(qi_ref)], axis=0)
    wt = jnp.transpose(wi_ref[...])[HEAD_DIM:HEAD_DIM + SUBLANES, :]

    npairs = (nkb + 1) // 2

    def score_block(j):
        kib = ki_ref[pl.ds(pl.multiple_of(j * tk, tk), tk), :]
        lg = lax.dot_general(kib, qis, _NT, preferred_element_type=F32)
        sc = jnp.maximum(lg[:, 0:tq], 0.0) * wt[0:1, :]
        for h in range(1, IDX_HEADS):
            sc = sc + jnp.maximum(lg[:, h * tq:(h + 1) * tq], 0.0) * wt[h:h + 1, :]
        bits = pltpu.bitcast(sc, jnp.int32)
        key = bits ^ ((bits >> 31) & 0x7FFFFFFF)
        key = jnp.where(j * tk + s_loc <= t_idx, key, INT_MIN)
        keys_ref[j] = key
        hi_ref[j] = (key >> 16).astype(jnp.int16)
        lo_ref[j] = ((key & 0xFFFF) - HALF16).astype(jnp.int16)

    def score_pair(t, carry):
        score_block(2 * t)
        score_block(2 * t + 1)
        return carry

    lax.fori_loop(0, npairs, score_pair, 0)

    nacc = tk // 4
    one, zero = jnp.asarray(1.0, BF16), jnp.asarray(0.0, BF16)

    def count16(d_ref, pred_fn):
        def chunks(j):
            r = jnp.where(pred_fn(d_ref[j]), one, zero).reshape(4, nacc, tq)
            return (r[0] + r[1]) + (r[2] + r[3])

        def pair(t, acc):
            return acc + (chunks(2 * t) + chunks(2 * t + 1))
        acc = lax.fori_loop(0, npairs, pair, jnp.zeros((nacc, tq), BF16))
        return jnp.sum(acc.astype(F32), axis=0, keepdims=True)

    def digit_search(d_ref, target):
        def bit_step(b, prefix):
            cand = prefix | lax.shift_left(jnp.int32(1), 15 - b)
            cand_s = (cand - HALF16).astype(jnp.int16)
            cnt = count16(d_ref, lambda d: d >= cand_s)
            return jnp.where(cnt >= target, cand, prefix)
        return lax.fori_loop(0, 16, bit_step, jnp.zeros((1, tq), jnp.int32))

    thr_hi = digit_search(hi_ref, topk) - HALF16
    thr_hi16 = thr_hi.astype(jnp.int16)
    c_hi = count16(hi_ref, lambda d: d > thr_hi16)

    def low_digit_pair(t, carry):
        for j in (2 * t, 2 * t + 1):
            lo_ref[j] = jnp.where(hi_ref[j] == thr_hi16, lo_ref[j], jnp.asarray(-HALF16, jnp.int16))
        return carry

    lax.fori_loop(0, npairs, low_digit_pair, 0)
    thr_lo = digit_search(lo_ref, topk - c_hi)
    thr_lo16 = (thr_lo - HALF16).astype(jnp.int16)
    c_gt = c_hi + count16(lo_ref, lambda d: d > thr_lo16)
    thr = lax.shift_left(thr_hi, 16) | thr_lo
    n_tie = jnp.where(thr == INT_MIN, 0.0, topk - c_gt)
    c_eq = count16(lo_ref, lambda d: d == thr_lo16)
    excess = jnp.max(jnp.where(thr == INT_MIN, 0.0, c_eq - n_tie))
    some_excess = excess > 0.5

    @pl.when(some_excess)
    def _():
        lower = jnp.where(lax.broadcasted_iota(jnp.int32, (tk, tk), 1)
                          < lax.broadcasted_iota(jnp.int32, (tk, tk), 0), 1.0, 0.0).astype(BF16)

        def select_block(j, off):
            kb = keys_ref[j]
            tie = kb == thr
            tie_f = jnp.where(tie, 1.0, 0.0)
            before = jnp.dot(lower, tie_f.astype(BF16), preferred_element_type=F32) + off
            bias_ref[j] = jnp.where(kb > thr, 0.0,
                                    jnp.where(tie, jnp.where(before < n_tie, 0.0, NEG), NEG))
            return off + jnp.sum(tie_f, axis=0, keepdims=True)

        def select_pair(t, off):
            return select_block(2 * t + 1, select_block(2 * t, off))

        lax.fori_loop(0, npairs, select_pair, jnp.zeros((1, tq), F32))

    @pl.when(jnp.logical_not(some_excess))
    def _():
        floor = jnp.where(thr == INT_MIN, INT_MIN + 1, thr)

        def select_pair(t, carry):
            for j in (2 * t, 2 * t + 1):
                bias_ref[j] = jnp.where(keys_ref[j] >= floor, 0.0, NEG)
            return carry

        lax.fori_loop(0, npairs, select_pair, 0)

    @pl.when(i == 0)
    def _():
        def block_norm(j, best):
            kf = ka_ref[pl.ds(pl.multiple_of(j * tk, tk), tk), :].astype(F32)
            return jnp.maximum(best, jnp.max(jnp.sum(kf * kf, axis=1, keepdims=True), axis=0, keepdims=True))
        k2 = lax.fori_loop(0, ka_ref.shape[0] // tk, block_norm, jnp.zeros((1, 1), F32))
        knorm_ref[...] = jnp.broadcast_to(k2, knorm_ref.shape)

    kmax = jnp.sqrt(knorm_ref[0:1, 0:1]) * BOUND_MARGIN
    slope = kmax * (-0.5 / Q_NORM_GUESS)
    offset = kmax * (-0.5 * Q_NORM_GUESS)
    shift_lane = lax.broadcasted_iota(jnp.int32, (tq, LANES), 1) == HEAD_DIM
    plain, shifted = [], []
    for qf in split_heads(q8_ref):
        neg_bound = jnp.sum(qf * qf, axis=1, keepdims=True) * slope + offset
        plain.append(qf.astype(BF16))
        shifted.append(jnp.where(shift_lane, neg_bound, qf).astype(BF16))
    qs_plain = jnp.concatenate(plain, axis=0)
    qs = jnp.concatenate(shifted, axis=0)
    acc_ref[...] = jnp.zeros(acc_ref.shape, F32)

    last = 2 * npairs - 1

    def masked_scores(q_stack, jc):
        rows = pl.ds(pl.multiple_of(jc * tk, tk), tk)
        s = lax.dot_general(ka_ref[rows, :], q_stack, _NT, preferred_element_type=F32)
        return [s[:, h * tq:(h + 1) * tq] + bias_ref[jc] for h in range(DSA_HEADS)]

    def scores_stage(s_ref, jb):
        for h, sh in enumerate(masked_scores(qs, jnp.minimum(jb, last))):
            s_ref[:, h * tq:(h + 1) * tq] = sh

    def exp_stage(s_ref, p_ref):
        for h in range(DSA_HEADS):
            hs = slice(h * tq, (h + 1) * tq)
            p_ref[:, hs] = jnp.exp2(s_ref[:, hs]).astype(BF16)

    def value_stage(p_ref, jb):
        acc_ref[...] += jnp.dot(vet_ref[jnp.clip(jb, 0, last)], p_ref[...], preferred_element_type=F32)

    scores_stage(s0_ref, 0)
    p1_ref[...] = jnp.zeros(p1_ref.shape, BF16)

    def attend_pair(t, carry):
        j = 2 * t
        scores_stage(s1_ref, j + 1)
        exp_stage(s0_ref, p0_ref)
        value_stage(p1_ref, j - 1)
        scores_stage(s0_ref, j + 2)
        exp_stage(s1_ref, p1_ref)
        value_stage(p0_ref, j)
        return carry

    lax.fori_loop(0, npairs, attend_pair, 0)
    value_stage(p1_ref, last)

    denom = acc_ref[HEAD_DIM:HEAD_DIM + 1, :]
    underflow = jnp.max(jnp.where(denom > TINY_DENOM, 0.0, 1.0)) > 0.5

    @pl.when(underflow)
    def _():
        m_ref[...] = jnp.full(m_ref.shape, NEG, F32)
        acc_ref[...] = jnp.zeros(acc_ref.shape, F32)

        def exact_block(j, carry):
            for h, sh in enumerate(masked_scores(qs_plain, j)):
                hs = slice(h * tq, (h + 1) * tq)
                m_old = m_ref[h:h + 1, :]
                m_new = jnp.maximum(m_old, jnp.max(sh, axis=0, keepdims=True))
                alpha_ref[h:h + 1, :] = jnp.exp2(m_old - m_new)
                m_ref[h:h + 1, :] = m_new
                p0_ref[:, hs] = jnp.exp2(sh - m_new).astype(BF16)
            pv = jnp.dot(vet_ref[j], p0_ref[...], preferred_element_type=F32)
            for h in range(DSA_HEADS):
                hs = slice(h * tq, (h + 1) * tq)
                acc_ref[:, hs] = acc_ref[:, hs] * alpha_ref[h:h + 1, :] + pv[:, hs]
            return carry

        lax.fori_loop(0, 2 * npairs, exact_block, 0)

    def head_out(h):
        a = acc_ref[:, h * tq:(h + 1) * tq]
        return a[0:HEAD_DIM, :] / a[HEAD_DIM:HEAD_DIM + 1, :]

    pairs = [jnp.transpose(jnp.concatenate([head_out(2 * g), head_out(2 * g + 1)], axis=0))
             for g in range(DSA_HEADS // 2)]
    yb = jnp.concatenate(pairs, axis=1) * _silu(bg_ref[...])
    z = zp_ref[...] + jnp.dot(yb.astype(BF16), wout_ref[...], preferred_element_type=F32)
    o_ref[...] = _layer_norm(z, g_ref[...], b_ref[...])


def _dsa(q8, qi, wi, bg, zp, ka, vet, ki, w_out_b, ln_g, ln_b, *, batch, seq, tq, tk):
    n = q8.shape[0]
    nq = seq // tq
    nk = seq // tk
    topk = min(TOPK_MAX, seq // 4)
    qrow = lambda b, i: (b * nq + i, 0)
    krow = lambda b, i: (b, 0)
    qtile = lambda width: pl.BlockSpec((tq, width), qrow)
    kernel = functools.partial(_dsa_kernel, tq=tq, tk=tk, topk=float(topk))
    return pl.pallas_call(
        kernel,
        name="dsa_attention",
        grid=(batch, nq),
        in_specs=[qtile(DSA_WIDTH), qtile(IDX_HEADS * IDX_DIM), qtile(LANES), qtile(DSA_WIDTH),
                  qtile(D_MODEL),
                  pl.BlockSpec((seq, LANES), krow),
                  pl.BlockSpec((nk, VE_ROWS, tk), lambda b, i: (b, 0, 0)),
                  pl.BlockSpec((seq, LANES), krow),
                  _resident((DSA_WIDTH, D_MODEL)), _resident((1, D_MODEL)), _resident((1, D_MODEL))],
        out_specs=qtile(D_MODEL),
        out_shape=jax.ShapeDtypeStruct((n, D_MODEL), F32),
        scratch_shapes=[pltpu.VMEM((nk, tk, tq), jnp.int32),
                        pltpu.VMEM((nk, tk, tq), jnp.int16),
                        pltpu.VMEM((nk, tk, tq), jnp.int16),
                        pltpu.VMEM((nk, tk, tq), F32),
                        pltpu.VMEM((tk, DSA_HEADS * tq), F32),
                        pltpu.VMEM((tk, DSA_HEADS * tq), F32),
                        pltpu.VMEM((tk, DSA_HEADS * tq), BF16),
                        pltpu.VMEM((tk, DSA_HEADS * tq), BF16),
                        pltpu.VMEM((VE_ROWS, DSA_HEADS * tq), F32),
                        pltpu.VMEM((DSA_HEADS, tq), F32),
                        pltpu.VMEM((DSA_HEADS, tq), F32),
                        pltpu.VMEM((SUBLANES, LANES), F32)],
        compiler_params=_cparams(2),
    )(q8, qi, wi, bg, zp, ka, vet, ki, w_out_b, ln_g.reshape(1, D_MODEL), ln_b.reshape(1, D_MODEL))


def _odd_kernel(x_ref, win_ref, wout_ref, kv_ref, vg_ref, vb_ref, ws_ref, bs_ref, g_ref, b_ref, o_ref,
                vn_ref, ug_ref, y_ref, *, tt):
    x = x_ref[...]
    xb = x.astype(BF16)

    def proj(col, width):
        return jnp.dot(xb, win_ref[:, col:col + width], preferred_element_type=F32)

    vn_ref[...] = _layer_norm(_gelu_tanh(proj(SG_WIDTH, SG_WIDTH)), vg_ref[...], vb_ref[...]).astype(BF16)
    ug_ref[...] = _gelu_tanh(proj(0, SG_WIDTH)) * _silu(proj(2 * SG_WIDTH, SG_WIDTH))
    causal = (lax.broadcasted_iota(jnp.int32, (SG_CHUNK, SG_CHUNK), 1)
              <= lax.broadcasted_iota(jnp.int32, (SG_CHUNK, SG_CHUNK), 0))
    gw = SG_WIDTH // SG_GROUPS
    for g in range(SG_GROUPS):
        w = jnp.where(causal, ws_ref[g], 0.0).astype(BF16)
        bias = bs_ref[:, g:g + 1]
        cs = slice(gw * g, gw * (g + 1))
        for c in range(tt // SG_CHUNK):
            rs = slice(SG_CHUNK * c, SG_CHUNK * (c + 1))
            mixed = jnp.dot(w, vn_ref[rs, cs], preferred_element_type=F32) + bias
            y_ref[rs, cs] = (ug_ref[rs, cs] * mixed).astype(BF16)
    z = DN_ALPHA * x + jnp.dot(y_ref[...], wout_ref[0:SG_WIDTH, :], preferred_element_type=F32)
    ym = _mem_heads(proj(3 * SG_WIDTH, MEM_WIDTH), kv_ref) * _silu(proj(3 * SG_WIDTH + MEM_WIDTH, MEM_WIDTH))
    z = z + jnp.dot(ym.astype(BF16), wout_ref[SG_WIDTH:SG_WIDTH + MEM_WIDTH, :], preferred_element_type=F32)
    o_ref[...] = _layer_norm(z, g_ref[...], b_ref[...])


def _odd_layer(x, mem_kv, vln_g, vln_b, ws, bs, w_in, w_out, ln_g, ln_b, *, seq, mem_len, tt):
    n = x.shape[0]
    nt = seq // tt
    row = lambda i: (i, 0)
    vec = lambda v: v.reshape(1, -1)
    return pl.pallas_call(
        functools.partial(_odd_kernel, tt=tt),
        name="odd_layer",
        grid=(n // tt,),
        in_specs=[pl.BlockSpec((tt, D_MODEL), row),
                  _resident(w_in.shape),
                  _resident(w_out.shape),
                  pl.BlockSpec((mem_len, 2 * MEM_WIDTH), lambda i: (i // nt, 0)),
                  _resident((1, SG_WIDTH)), _resident((1, SG_WIDTH)),
                  _resident((SG_GROUPS, SG_CHUNK, SG_CHUNK)),
                  _resident((SG_CHUNK, SG_GROUPS)),
                  _resident((1, D_MODEL)), _resident((1, D_MODEL))],
        out_specs=pl.BlockSpec((tt, D_MODEL), row),
        out_shape=jax.ShapeDtypeStruct((n, D_MODEL), F32),
        scratch_shapes=[pltpu.VMEM((tt, SG_WIDTH), BF16),
                        pltpu.VMEM((tt, SG_WIDTH), F32),
                        pltpu.VMEM((tt, SG_WIDTH), BF16)],
        compiler_params=_cparams(1),
    )(x, w_in.astype(BF16), w_out.astype(BF16), mem_kv, vec(vln_g), vec(vln_b), ws, jnp.transpose(bs),
      vec(ln_g), vec(ln_b))


def _even_layer(x, mem_kv, cosf, sinf, w_in, conv_w, conv_b, cln_g, cln_b, pw2_w, pw2_b, w_out, ln_g, ln_b,
                *, batch, seq, mem_len, tt):
    tk = min(K_TILE, seq)
    zp, bg, q8, qi, wi, ka, vet, ki = _even_front(x, mem_kv, cosf, sinf, w_in, conv_w, conv_b, cln_g, cln_b,
                                                  pw2_w, pw2_b, w_out, batch=batch, seq=seq, mem_len=mem_len,
                                                  tt=tt, tk=tk)
    w_out_b = w_out[CONV_WIDTH:CONV_WIDTH + DSA_WIDTH].astype(BF16)
    return _dsa(q8, qi, wi, bg, zp, ka, vet, ki, w_out_b, ln_g, ln_b, batch=batch, seq=seq, tq=Q_TILE, tk=tk)


def kernel(x, mem, positions, e_w_in, e_conv_w, e_conv_b, e_cln_g, e_cln_b, e_pw2_w, e_pw2_b, e_w_out, o_w_in, o_vln_g, o_vln_b, o_ws, o_bs, o_w_out, mem_wk, mem_wv, ln_g, ln_b):
    batch, seq, d = x.shape
    mem_len = mem.shape[1]
    tt = min(TOKEN_TILE, seq)
    cosf, sinf = _rope_tables(positions)
    xf = x.reshape(batch * seq, d)
    memf = mem.reshape(batch * mem_len, d)
    for layer in range(DEPTH):
        j = layer // 2
        w_kv = jnp.concatenate([mem_wk[layer], mem_wv[layer]], axis=1).astype(BF16)
        mem_kv = _matmul(memf, w_kv, tm=min(tt, memf.shape[0]), tn=1024, out_dtype=BF16)
        if layer % 2 == 0:
            xf = _even_layer(xf, mem_kv, cosf, sinf, _even_w_in(e_w_in, j), e_conv_w[j], e_conv_b[j], e_cln_g[j], e_cln_b[j],
                             e_pw2_w[j], e_pw2_b[j], e_w_out[j], ln_g[layer], ln_b[layer],
                             batch=batch, seq=seq, mem_len=mem_len, tt=tt)
        else:
            xf = _odd_layer(xf, mem_kv, o_vln_g[j], o_vln_b[j], o_ws[j], o_bs[j], o_w_in[j], o_w_out[j],
                            ln_g[layer], ln_b[layer], seq=seq, mem_len=mem_len, tt=tt)
    return xf.reshape(batch, seq, d)
```

```python
import functools

import numpy as np
import jax
import jax.numpy as jnp
from jax import lax
from jax.experimental import pallas as pl
from jax.experimental.pallas import tpu as pltpu

F32 = jnp.float32
BF16 = jnp.bfloat16

D_MODEL = 1024
DEPTH = 4
HEAD_DIM = 64
CONV_WIDTH = 512
CONV_K = 31
DSA_HEADS = 8
DSA_WIDTH = 512
IDX_HEADS = 4
IDX_DIM = 64
TOPK_MAX = 256
SG_CHUNK = 128
SG_GROUPS = 8
SG_WIDTH = 1024
MEM_HEADS = 4
MEM_HEAD_DIM = 128
MEM_WIDTH = 512
ROPE_THETA = 10000.0
LN_EPS = 1e-5
DN_ALPHA = (2 * DEPTH) ** 0.25

LANES = 128
SUBLANES = 8
VE_ROWS = HEAD_DIM + 16
CONV_HALO = 32
NEG = -1e30
INT_MIN = -(2 ** 31)
HALF16 = 2 ** 15
QK_SCALE_LOG2 = float(np.log2(np.e)) * HEAD_DIM ** -0.5
BOUND_MARGIN = 1.0 + 2.0 ** -8
Q_NORM_GUESS = QK_SCALE_LOG2 * HEAD_DIM ** 0.5
TINY_DENOM = 2.0 ** -100

VMEM_LIMIT = 56 * 1024 * 1024

TOKEN_TILE = 512
ODD_TOKEN_TILE = 1024
Q_TILE = 256
K_TILE = 256

_NT = (((1,), (1,)), ((), ()))


def _cparams(n_axes, flags=None):
    return pltpu.CompilerParams(dimension_semantics=("arbitrary",) * n_axes,
                                vmem_limit_bytes=VMEM_LIMIT, flags=flags)


def _resident(shape):
    zeros = (0,) * len(shape)
    return pl.BlockSpec(shape, lambda *_: zeros, pipeline_mode=pl.Buffered(1))


def _silu(g):
    return g * (1.0 / (1.0 + jnp.exp(-g)))


def _gelu_tanh(x):
    c = np.float32(np.sqrt(2.0 / np.pi))
    return x * (0.5 * (1.0 + jnp.tanh(c * (x + 0.044715 * (x * x * x)))))


def _layer_norm(x, g, b):
    mu = jnp.mean(x, axis=-1, keepdims=True)
    xc = x - mu
    var = jnp.mean(xc * xc, axis=-1, keepdims=True)
    return xc * lax.rsqrt(var + LN_EPS) * g + b


def _mem_heads(mq, kv_ref):
    outs = []
    for h in range(MEM_HEADS):
        hs = slice(MEM_HEAD_DIM * h, MEM_HEAD_DIM * (h + 1))
        vs = slice(MEM_WIDTH + MEM_HEAD_DIM * h, MEM_WIDTH + MEM_HEAD_DIM * (h + 1))
        s = lax.dot_general(mq[:, hs].astype(BF16), kv_ref[:, hs], _NT,
                            preferred_element_type=F32) * (MEM_HEAD_DIM ** -0.5)
        e = jnp.exp(s - jnp.max(s, axis=-1, keepdims=True))
        p = e / jnp.sum(e, axis=-1, keepdims=True)
        outs.append(jnp.dot(p.astype(BF16), kv_ref[:, vs], preferred_element_type=F32))
    return jnp.concatenate(outs, axis=1)


def _mm_kernel(x_ref, w_ref, o_ref):
    o_ref[...] = jnp.dot(x_ref[...].astype(BF16), w_ref[...],
                         preferred_element_type=F32).astype(o_ref.dtype)


def _matmul(x, w, *, tm, tn, out_dtype=F32):
    m, k = x.shape
    n = w.shape[1]
    return pl.pallas_call(
        _mm_kernel,
        name="mem_kv_proj",
        grid=(m // tm, n // tn),
        in_specs=[pl.BlockSpec((tm, k), lambda i, j: (i, 0)),
                  pl.BlockSpec((k, tn), lambda i, j: (0, j))],
        out_specs=pl.BlockSpec((tm, tn), lambda i, j: (i, j)),
        out_shape=jax.ShapeDtypeStruct((m, n), out_dtype),
        compiler_params=_cparams(2),
    )(x, w)


def _rope_table_kernel(pos_ref, inv_ref, cos_ref, sin_ref):
    ang = pos_ref[...].astype(F32) * inv_ref[...]
    half = HEAD_DIM // 2
    groups = LANES // half
    grp = lax.broadcasted_iota(jnp.int32, ang.shape, 1) // half
    sign = jnp.where((grp & 1) == 0, -1.0, 1.0)
    for table, out_ref, scale in ((jnp.cos(ang), cos_ref, None), (jnp.sin(ang), sin_ref, sign)):
        rolled = [table] + [pltpu.roll(table, half * k, 1) for k in range(1, groups)]
        for q in range(groups):
            y = rolled[(0 - q) % groups]
            for g in range(1, groups):
                y = jnp.where(grp == g, rolled[(g - q) % groups], y)
            out_ref[q] = y if scale is None else y * scale


def _rope_tables(positions):
    n = positions.size
    half = HEAD_DIM // 2
    per_row = LANES // half
    rows = n // per_row
    inv = (ROPE_THETA ** (-jnp.arange(0, HEAD_DIM, 2, dtype=F32) / HEAD_DIM)).astype(F32)
    pos_rep = jnp.repeat(jnp.transpose(positions.reshape(per_row, rows)), half, axis=1)
    inv_rep = jnp.tile(inv, per_row).reshape(1, LANES)
    tr = min(512, rows)
    cos, sin = pl.pallas_call(
        _rope_table_kernel,
        name="rope_tables",
        grid=(rows // tr,),
        in_specs=[pl.BlockSpec((tr, LANES), lambda i: (i, 0)),
                  pl.BlockSpec((1, LANES), lambda i: (0, 0))],
        out_specs=[pl.BlockSpec((per_row, tr, LANES), lambda i: (0, i, 0))] * 2,
        out_shape=[jax.ShapeDtypeStruct((per_row, rows, LANES), F32)] * 2,
        compiler_params=_cparams(1),
    )(pos_rep, inv_rep)
    return cos.reshape(n, LANES), sin.reshape(n, LANES)


_E_GATE_B = 3 * CONV_WIDTH
_E_MQ = _E_GATE_B + DSA_WIDTH
_E_MGATE = _E_MQ + MEM_WIDTH
_E_Q = _E_MGATE + MEM_WIDTH
_E_QI = _E_Q + DSA_WIDTH
_E_KKI = _E_QI + IDX_HEADS * IDX_DIM
_E_VW = _E_KKI + LANES
_E_TOTAL = _E_VW + LANES


_E_SPLIT_NAMES = ("a_val", "a_glu", "a_gate", "q", "k", "v", "qi", "ki", "wi", "b_gate", "mq", "m_gate")
_E_SPLIT_SIZES = (CONV_WIDTH, CONV_WIDTH, CONV_WIDTH, DSA_WIDTH, HEAD_DIM, HEAD_DIM,
                  IDX_HEADS * IDX_DIM, IDX_DIM, IDX_HEADS, DSA_WIDTH, MEM_WIDTH, MEM_WIDTH)
_E_SPLIT = dict(zip(_E_SPLIT_NAMES, zip(np.cumsum((0,) + _E_SPLIT_SIZES[:-1]).tolist(), _E_SPLIT_SIZES)))
_E_ORDER = ("a_val", "a_glu", "a_gate", "b_gate", "mq", "m_gate", "q", "qi", "k", "ki", "v", "wi")


def _even_w_layout_kernel(w_ref, o_ref):
    col = 0
    for name in _E_ORDER:
        src, width = _E_SPLIT[name]
        o_ref[:, col:col + width] = w_ref[:, src:src + width].astype(BF16)
        col += width
    o_ref[:, col:_E_TOTAL] = jnp.zeros((o_ref.shape[0], _E_TOTAL - col), BF16)


def _even_w_in(w_stack, layer):
    _, d, n_in = w_stack.shape
    tr = 256
    return pl.pallas_call(
        _even_w_layout_kernel,
        name="even_w_layout",
        grid=(d // tr,),
        in_specs=[pl.BlockSpec((None, tr, n_in), lambda i: (layer, i, 0))],
        out_specs=pl.BlockSpec((tr, _E_TOTAL), lambda i: (i, 0)),
        out_shape=jax.ShapeDtypeStruct((d, _E_TOTAL), BF16),
        compiler_params=_cparams(1),
    )(w_stack)


def _even_front_kernel(x_ref, win_ref, wout_ref, kv_ref, cw_ref, cb_ref, cg_ref, cbeta_ref, pw_ref, pb_ref,
                       cos_ref, sin_ref,
                       zp_ref, bg_ref, q8_ref, qio_ref, wi_ref, ka_ref, vet_ref, ki_ref,
                       hist_ref, shift_ref, *, tt, tk):
    @pl.when(pl.program_id(1) == 0)
    def _():
        hist_ref[0:CONV_HALO, :] = jnp.zeros((CONV_HALO, CONV_WIDTH), F32)

    x = x_ref[...]
    xb = x.astype(BF16)

    def proj(col, width):
        return jnp.dot(xb, win_ref[:, col:col + width], preferred_element_type=F32)

    glu = proj(CONV_WIDTH, CONV_WIDTH)
    hist_ref[CONV_HALO:CONV_HALO + tt, :] = proj(0, CONV_WIDTH) * (1.0 / (1.0 + jnp.exp(-glu)))
    span = tt + CONV_HALO - SUBLANES
    for r in range(1, SUBLANES):
        shift_ref[r - 1, 0:span, :] = hist_ref[pl.ds(r, span), :]
    base = CONV_HALO - (CONV_K - 1)
    acc = None
    for j in range(CONV_K):
        a, r = divmod(base + j, SUBLANES)
        rows = slice(SUBLANES * a, SUBLANES * a + tt)
        src = hist_ref[rows, :] if r == 0 else shift_ref[r - 1, rows, :]
        term = src * cw_ref[j:j + 1, :]
        acc = term if acc is None else acc + term
    acc = acc + cb_ref[...]
    hist_ref[0:CONV_HALO, :] = hist_ref[tt:tt + CONV_HALO, :]
    y = _silu(_layer_norm(acc, cg_ref[...], cbeta_ref[...]))
    y = jnp.dot(y.astype(BF16), pw_ref[...], preferred_element_type=F32) + pb_ref[...]
    ya = y * _silu(proj(2 * CONV_WIDTH, CONV_WIDTH))
    z = DN_ALPHA * x + jnp.dot(ya.astype(BF16), wout_ref[0:CONV_WIDTH, :], preferred_element_type=F32)

    ym = _mem_heads(proj(_E_MQ, MEM_WIDTH), kv_ref) * _silu(proj(_E_MGATE, MEM_WIDTH))
    m0 = CONV_WIDTH + DSA_WIDTH
    zp_ref[...] = z + jnp.dot(ym.astype(BF16), wout_ref[m0:m0 + MEM_WIDTH, :], preferred_element_type=F32)

    bg_ref[...] = proj(_E_GATE_B, DSA_WIDTH)
    cosf = cos_ref[...]
    sinf = sin_ref[...]

    lane = lax.broadcasted_iota(jnp.int32, (tt, LANES), 1)
    first_half = (lane & (HEAD_DIM - 1)) < HEAD_DIM // 2
    low = lane < HEAD_DIM
    marker = jnp.where(lane == HEAD_DIM, 1.0, 0.0)

    def rope(g):
        rot = jnp.where(first_half, pltpu.roll(g, LANES - HEAD_DIM // 2, 1), pltpu.roll(g, HEAD_DIM // 2, 1))
        return g * cosf + rot * sinf

    qf = proj(_E_Q, DSA_WIDTH)
    for g in range(DSA_WIDTH // LANES):
        sl = slice(LANES * g, LANES * (g + 1))
        q8_ref[:, sl] = (rope(qf[:, sl]) * QK_SCALE_LOG2).astype(BF16)
    qif = proj(_E_QI, IDX_HEADS * IDX_DIM)
    for g in range(IDX_HEADS * IDX_DIM // LANES):
        sl = slice(LANES * g, LANES * (g + 1))
        qio_ref[:, sl] = rope(qif[:, sl]).astype(BF16)
    kk = rope(proj(_E_KKI, LANES))
    ka_ref[...] = jnp.where(low, kk, marker).astype(BF16)
    ki_ref[...] = jnp.where(low, pltpu.roll(kk, HEAD_DIM, 1), 0.0).astype(BF16)
    vw = proj(_E_VW, LANES)
    ve = jnp.where(low, vw, marker)
    for c in range(tt // tk):
        vet_ref[c] = jnp.transpose(ve[tk * c:tk * (c + 1), :])[0:VE_ROWS, :].astype(BF16)
    wi_ref[...] = vw * ((IDX_HEADS ** -0.5) * (IDX_DIM ** -0.5))


def _even_front(x, mem_kv, cosf, sinf, w_in, conv_w, conv_b, cln_g, cln_b, pw_w, pw_b, w_out,
                *, batch, seq, mem_len, tt, tk):
    n = x.shape[0]
    nt = seq // tt
    wq = DSA_WIDTH
    wqi = IDX_HEADS * IDX_DIM
    cw = jnp.concatenate([conv_w, jnp.zeros((CONV_HALO - CONV_K, CONV_WIDTH), F32)], axis=0)
    vec = lambda v: v.reshape(1, -1)
    row = lambda b, i: (b * nt + i, 0)
    tile = lambda width: pl.BlockSpec((tt, width), row)
    return pl.pallas_call(
        functools.partial(_even_front_kernel, tt=tt, tk=tk),
        name="even_front",
        grid=(batch, nt),
        in_specs=[tile(D_MODEL),
                  _resident((D_MODEL, _E_TOTAL)),
                  _resident(w_out.shape),
                  pl.BlockSpec((mem_len, 2 * MEM_WIDTH), lambda b, i: (b, 0)),
                  _resident((CONV_HALO, CONV_WIDTH)),
                  _resident((1, CONV_WIDTH)), _resident((1, CONV_WIDTH)), _resident((1, CONV_WIDTH)),
                  _resident((CONV_WIDTH, CONV_WIDTH)),
                  _resident((1, CONV_WIDTH)),
                  tile(LANES), tile(LANES)],
        out_specs=[tile(D_MODEL), tile(DSA_WIDTH), tile(wq), tile(wqi), tile(LANES), tile(LANES),
                   pl.BlockSpec((tt // tk, VE_ROWS, tk), lambda b, i: (b * nt + i, 0, 0)),
                   tile(LANES)],
        out_shape=[jax.ShapeDtypeStruct((n, D_MODEL), F32), jax.ShapeDtypeStruct((n, DSA_WIDTH), F32),
                   jax.ShapeDtypeStruct((n, wq), BF16), jax.ShapeDtypeStruct((n, wqi), BF16),
                   jax.ShapeDtypeStruct((n, LANES), F32), jax.ShapeDtypeStruct((n, LANES), BF16),
                   jax.ShapeDtypeStruct((n // tk, VE_ROWS, tk), BF16), jax.ShapeDtypeStruct((n, LANES), BF16)],
        scratch_shapes=[pltpu.VMEM((CONV_HALO + tt, CONV_WIDTH), F32),
                        pltpu.VMEM((SUBLANES - 1, CONV_HALO + tt - SUBLANES, CONV_WIDTH), F32)],
        compiler_params=_cparams(2),
    )(x, w_in, w_out.astype(BF16), mem_kv, cw, vec(conv_b), vec(cln_g), vec(cln_b),
      pw_w.astype(BF16), vec(pw_b), cosf, sinf)


def _dsa_kernel(q8_ref, qi_ref, wi_ref, bg_ref, zp_ref, ka_ref, vet_ref, ki_ref, wout_ref, g_ref, b_ref, o_ref,
                keys_ref, hi_ref, lo_ref, bias_ref, s0_ref, s1_ref, p0_ref, p1_ref, acc_ref, m_ref,
                alpha_ref, knorm_ref, *, tq, tk, topk):
    i = pl.program_id(1)
    nkb = (i * tq + tq + tk - 1) // tk
    s_loc = lax.broadcasted_iota(jnp.int32, (tk, tq), 0)
    t_idx = i * tq + lax.broadcasted_iota(jnp.int32, (tk, tq), 1)

    low = lax.broadcasted_iota(jnp.int32, (tq, LANES), 1) < HEAD_DIM

    def split_heads(ref):
        heads = []
        for g in range(ref.shape[1] // LANES):
            two = ref[:, LANES * g:LANES * (g + 1)].astype(F32)
            heads += [jnp.where(low, two, 0.0), jnp.where(low, pltpu.roll(two, HEAD_DIM, 1), 0.0)]
        return heads

    qis = jnp.concatenate([qh.astype(BF16) for qh in split_heads(qi_ref)], axis=0)
    wt = jnp.transpose(wi_ref[...])[HEAD_DIM:HEAD_DIM + SUBLANES, :]

    npairs = (nkb + 1) // 2

    def score_block(j):
        kib = ki_ref[pl.ds(pl.multiple_of(j * tk, tk), tk), :]
        lg = lax.dot_general(kib, qis, _NT, preferred_element_type=F32)
        sc = jnp.maximum(lg[:, 0:tq], 0.0) * wt[0:1, :]
        for h in range(1, IDX_HEADS):
            sc = sc + jnp.maximum(lg[:, h * tq:(h + 1) * tq], 0.0) * wt[h:h + 1, :]
        bits = pltpu.bitcast(sc, jnp.int32)
        key = bits ^ ((bits >> 31) & 0x7FFFFFFF)
        key = jnp.where(j * tk + s_loc <= t_idx, key, INT_MIN)
        keys_ref[j] = key
        hi_ref[j] = (key >> 16).astype(jnp.int16)
        lo_ref[j] = ((key & 0xFFFF) - HALF16).astype(jnp.int16)

    def score_pair(t, carry):
        score_block(2 * t)
        score_block(2 * t + 1)
        return carry

    lax.fori_loop(0, npairs, score_pair, 0)

    nacc = tk // 4
    one, zero = jnp.asarray(1.0, BF16), jnp.asarray(0.0, BF16)

    def count16(d_ref, pred_fn):
        def chunks(j):
            r = jnp.where(pred_fn(d_ref[j]), one, zero).reshape(4, nacc, tq)
            return (r[0] + r[1]) + (r[2] + r[3])

        def pair(t, acc):
            return acc + (chunks(2 * t) + chunks(2 * t + 1))
        acc = lax.fori_loop(0, npairs, pair, jnp.zeros((nacc, tq), BF16))
        return jnp.sum(acc.astype(F32), axis=0, keepdims=True)

    def digit_search(d_ref, target):
        def bit_step(b, prefix):
            cand = prefix | lax.shift_left(jnp.int32(1), 15 - b)
            cand_s = (cand - HALF16).astype(jnp.int16)
            cnt = count16(d_ref, lambda d: d >= cand_s)
            return jnp.where(cnt >= target, cand, prefix)
        return lax.fori_loop(0, 16, bit_step, jnp.zeros((1, tq), jnp.int32))

    thr_hi = digit_search(hi_ref, topk) - HALF16
    thr_hi16 = thr_hi.astype(jnp.int16)
    c_hi = count16(hi_ref, lambda d: d > thr_hi16)

    def low_digit_pair(t, carry):
        for j in (2 * t, 2 * t + 1):
            lo_ref[j] = jnp.where(hi_ref[j] == thr_hi16, lo_ref[j], jnp.asarray(-HALF16, jnp.int16))
        return carry

    lax.fori_loop(0, npairs, low_digit_pair, 0)
    thr_lo = digit_search(lo_ref, topk - c_hi)
    thr_lo16 = (thr_lo - HALF16).astype(jnp.int16)
    c_gt = c_hi + count16(lo_ref, lambda d: d > thr_lo16)
    thr = lax.shift_left(thr_hi, 16) | thr_lo
    n_tie = jnp.where(thr == INT_MIN, 0.0, topk - c_gt)
    c_eq = count16(lo_ref, lambda d: d == thr_lo16)
    excess = jnp.max(jnp.where(thr == INT_MIN, 0.0, c_eq - n_tie))
    some_excess = excess > 0.5

    @pl.when(some_excess)
    def _():
        lower = jnp.where(lax.broadcasted_iota(jnp.int32, (tk, tk), 1)
                          < lax.broadcasted_iota(jnp.int32, (tk, tk), 0), 1.0, 0.0).astype(BF16)

        def select_block(j, off):
            kb = keys_ref[j]
            tie = kb == thr
            tie_f = jnp.where(tie, 1.0, 0.0)
            before = jnp.dot(lower, tie_f.astype(BF16), preferred_element_type=F32) + off
            bias_ref[j] = jnp.where(kb > thr, 0.0,
                                    jnp.where(tie, jnp.where(before < n_tie, 0.0, NEG), NEG))
            return off + jnp.sum(tie_f, axis=0, keepdims=True)

        def select_pair(t, off):
            return select_block(2 * t + 1, select_block(2 * t, off))

        lax.fori_loop(0, npairs, select_pair, jnp.zeros((1, tq), F32))

    @pl.when(jnp.logical_not(some_excess))
    def _():
        floor = jnp.where(thr == INT_MIN, INT_MIN + 1, thr)

        def select_pair(t, carry):
            for j in (2 * t, 2 * t + 1):
                bias_ref[j] = jnp.where(keys_ref[j] >= floor, 0.0, NEG)
            return carry

        lax.fori_loop(0, npairs, select_pair, 0)

    @pl.when(i == 0)
    def _():
        def block_norm(j, best):
            kf = ka_ref[pl.ds(pl.multiple_of(j * tk, tk), tk), :].astype(F32)
            return jnp.maximum(best, jnp.max(jnp.sum(kf * kf, axis=1, keepdims=True), axis=0, keepdims=True))
        k2 = lax.fori_loop(0, ka_ref.shape[0] // tk, block_norm, jnp.zeros((1, 1), F32))
        knorm_ref[...] = jnp.broadcast_to(k2, knorm_ref.shape)

    kmax = jnp.sqrt(knorm_ref[0:1, 0:1]) * BOUND_MARGIN
    slope = kmax * (-0.5 / Q_NORM_GUESS)
    offset = kmax * (-0.5 * Q_NORM_GUESS)
    shift_lane = lax.broadcasted_iota(jnp.int32, (tq, LANES), 1) == HEAD_DIM
    plain, shifted = [], []
    for qf in split_heads(q8_ref):
        neg_bound = jnp.sum(qf * qf, axis=1, keepdims=True) * slope + offset
        plain.append(qf.astype(BF16))
        shifted.append(jnp.where(shift_lane, neg_bound, qf).astype(BF16))
    qs_plain = jnp.concatenate(plain, axis=0)
    qs = jnp.concatenate(shifted, axis=0)
    acc_ref[...] = jnp.zeros(acc_ref.shape, F32)

    last = 2 * npairs - 1

    def masked_scores(q_stack, jc):
        rows = pl.ds(pl.multiple_of(jc * tk, tk), tk)
        s = lax.dot_general(ka_ref[rows, :], q_stack, _NT, preferred_element_type=F32)
        return [s[:, h * tq:(h + 1) * tq] + bias_ref[jc] for h in range(DSA_HEADS)]

    def scores_stage(s_ref, jb):
        for h, sh in enumerate(masked_scores(qs, jnp.minimum(jb, last))):
            s_ref[:, h * tq:(h + 1) * tq] = sh

    def exp_stage(s_ref, p_ref):
        for h in range(DSA_HEADS):
            hs = slice(h * tq, (h + 1) * tq)
            p_ref[:, hs] = jnp.exp2(s_ref[:, hs]).astype(BF16)

    def value_stage(p_ref, jb):
        acc_ref[...] += jnp.dot(vet_ref[jnp.clip(jb, 0, last)], p_ref[...], preferred_element_type=F32)

    scores_stage(s0_ref, 0)
    p1_ref[...] = jnp.zeros(p1_ref.shape, BF16)

    def attend_pair(t, carry):
        j = 2 * t
        scores_stage(s1_ref, j + 1)
        exp_stage(s0_ref, p0_ref)
        value_stage(p1_ref, j - 1)
        scores_stage(s0_ref, j + 2)
        exp_stage(s1_ref, p1_ref)
        value_stage(p0_ref, j)
        return carry

    lax.fori_loop(0, npairs, attend_pair, 0)
    value_stage(p1_ref, last)

    denom = acc_ref[HEAD_DIM:HEAD_DIM + 1, :]
    underflow = jnp.max(jnp.where(denom > TINY_DENOM, 0.0, 1.0)) > 0.5

    @pl.when(underflow)
    def _():
        m_ref[...] = jnp.full(m_ref.shape, NEG, F32)
        acc_ref[...] = jnp.zeros(acc_ref.shape, F32)

        def exact_block(j, carry):
            for h, sh in enumerate(masked_scores(qs_plain, j)):
                hs = slice(h * tq, (h + 1) * tq)
                m_old = m_ref[h:h + 1, :]
                m_new = jnp.maximum(m_old, jnp.max(sh, axis=0, keepdims=True))
                alpha_ref[h:h + 1, :] = jnp.exp2(m_old - m_new)
                m_ref[h:h + 1, :] = m_new
                p0_ref[:, hs] = jnp.exp2(sh - m_new).astype(BF16)
            pv = jnp.dot(vet_ref[j], p0_ref[...], preferred_element_type=F32)
            for h in range(DSA_HEADS):
                hs = slice(h * tq, (h + 1) * tq)
                acc_ref[:, hs] = acc_ref[:, hs] * alpha_ref[h:h + 1, :] + pv[:, hs]
            return carry

        lax.fori_loop(0, 2 * npairs, exact_block, 0)

    def head_out(h):
        a = acc_ref[:, h * tq:(h + 1) * tq]
        return a[0:HEAD_DIM, :] / a[HEAD_DIM:HEAD_DIM + 1, :]

    pairs = [jnp.transpose(jnp.concatenate([head_out(2 * g), head_out(2 * g + 1)], axis=0))
             for g in range(DSA_HEADS // 2)]
    yb = jnp.concatenate(pairs, axis=1) * _silu(bg_ref[...])
    z = zp_ref[...] + jnp.dot(yb.astype(BF16), wout_ref[...], preferred_element_type=F32)
    o_ref[...] = _layer_norm(z, g_ref[...], b_ref[...])


def _dsa(q8, qi, wi, bg, zp, ka, vet, ki, w_out_b, ln_g, ln_b, *, batch, seq, tq, tk):
    n = q8.shape[0]
    nq = seq // tq
    nk = seq // tk
    topk = min(TOPK_MAX, seq // 4)
    qrow = lambda b, i: (b * nq + i, 0)
    krow = lambda b, i: (b, 0)
    qtile = lambda width: pl.BlockSpec((tq, width), qrow)
    kernel = functools.partial(_dsa_kernel, tq=tq, tk=tk, topk=float(topk))
    return pl.pallas_call(
        kernel,
        name="dsa_attention",
        grid=(batch, nq),
        in_specs=[qtile(DSA_WIDTH), qtile(IDX_HEADS * IDX_DIM), qtile(LANES), qtile(DSA_WIDTH),
                  qtile(D_MODEL),
                  pl.BlockSpec((seq, LANES), krow),
                  pl.BlockSpec((nk, VE_ROWS, tk), lambda b, i: (b, 0, 0)),
                  pl.BlockSpec((seq, LANES), krow),
                  _resident((DSA_WIDTH, D_MODEL)), _resident((1, D_MODEL)), _resident((1, D_MODEL))],
        out_specs=qtile(D_MODEL),
        out_shape=jax.ShapeDtypeStruct((n, D_MODEL), F32),
        scratch_shapes=[pltpu.VMEM((nk, tk, tq), jnp.int32),
                        pltpu.VMEM((nk, tk, tq), jnp.int16),
                        pltpu.VMEM((nk, tk, tq), jnp.int16),
                        pltpu.VMEM((nk, tk, tq), F32),
                        pltpu.VMEM((tk, DSA_HEADS * tq), F32),
                        pltpu.VMEM((tk, DSA_HEADS * tq), F32),
                        pltpu.VMEM((tk, DSA_HEADS * tq), BF16),
                        pltpu.VMEM((tk, DSA_HEADS * tq), BF16),
                        pltpu.VMEM((VE_ROWS, DSA_HEADS * tq), F32),
                        pltpu.VMEM((DSA_HEADS, tq), F32),
                        pltpu.VMEM((DSA_HEADS, tq), F32),
                        pltpu.VMEM((SUBLANES, LANES), F32)],
        compiler_params=_cparams(2),
    )(q8, qi, wi, bg, zp, ka, vet, ki, w_out_b, ln_g.reshape(1, D_MODEL), ln_b.reshape(1, D_MODEL))


def _odd_kernel(x_ref, win_ref, wout_ref, kv_ref, vg_ref, vb_ref, ws_ref, bs_ref, g_ref, b_ref, o_ref,
                vn_ref, ug_ref, y_ref, *, tt):
    x = x_ref[...]
    xb = x.astype(BF16)

    def proj(col, width):
        return jnp.dot(xb, win_ref[:, col:col + width], preferred_element_type=F32)

    vn_ref[...] = _layer_norm(_gelu_tanh(proj(SG_WIDTH, SG_WIDTH)), vg_ref[...], vb_ref[...]).astype(BF16)
    ug_ref[...] = _gelu_tanh(proj(0, SG_WIDTH)) * _silu(proj(2 * SG_WIDTH, SG_WIDTH))
    causal = (lax.broadcasted_iota(jnp.int32, (SG_CHUNK, SG_CHUNK), 1)
              <= lax.broadcasted_iota(jnp.int32, (SG_CHUNK, SG_CHUNK), 0))
    gw = SG_WIDTH // SG_GROUPS
    for g in range(SG_GROUPS):
        w = jnp.where(causal, ws_ref[g], 0.0).astype(BF16)
        bias = bs_ref[:, g:g + 1]
        cs = slice(gw * g, gw * (g + 1))
        for c in range(tt // SG_CHUNK):
            rs = slice(SG_CHUNK * c, SG_CHUNK * (c + 1))
            mixed = jnp.dot(w, vn_ref[rs, cs], preferred_element_type=F32) + bias
            y_ref[rs, cs] = (ug_ref[rs, cs] * mixed).astype(BF16)
    z = DN_ALPHA * x + jnp.dot(y_ref[...], wout_ref[0:SG_WIDTH, :], preferred_element_type=F32)
    ym = _mem_heads(proj(3 * SG_WIDTH, MEM_WIDTH), kv_ref) * _silu(proj(3 * SG_WIDTH + MEM_WIDTH, MEM_WIDTH))
    z = z + jnp.dot(ym.astype(BF16), wout_ref[SG_WIDTH:SG_WIDTH + MEM_WIDTH, :], preferred_element_type=F32)
    o_ref[...] = _layer_norm(z, g_ref[...], b_ref[...])


def _odd_layer(x, mem_kv, vln_g, vln_b, ws, bs, w_in, w_out, ln_g, ln_b, *, seq, mem_len, tt):
    n = x.shape[0]
    nt = seq // tt
    row = lambda i: (i, 0)
    vec = lambda v: v.reshape(1, -1)
    return pl.pallas_call(
        functools.partial(_odd_kernel, tt=tt),
        name="odd_layer",
        grid=(n // tt,),
        in_specs=[pl.BlockSpec((tt, D_MODEL), row),
                  _resident(w_in.shape),
                  _resident(w_out.shape),
                  pl.BlockSpec((mem_len, 2 * MEM_WIDTH), lambda i: (i // nt, 0)),
                  _resident((1, SG_WIDTH)), _resident((1, SG_WIDTH)),
                  _resident((SG_GROUPS, SG_CHUNK, SG_CHUNK)),
                  _resident((SG_CHUNK, SG_GROUPS)),
                  _resident((1, D_MODEL)), _resident((1, D_MODEL))],
        out_specs=pl.BlockSpec((tt, D_MODEL), row),
        out_shape=jax.ShapeDtypeStruct((n, D_MODEL), F32),
        scratch_shapes=[pltpu.VMEM((tt, SG_WIDTH), BF16),
                        pltpu.VMEM((tt, SG_WIDTH), F32),
                        pltpu.VMEM((tt, SG_WIDTH), BF16)],
        compiler_params=_cparams(1),
    )(x, w_in.astype(BF16), w_out.astype(BF16), mem_kv, vec(vln_g), vec(vln_b), ws, jnp.transpose(bs),
      vec(ln_g), vec(ln_b))


def _even_layer(x, mem_kv, cosf, sinf, w_in, conv_w, conv_b, cln_g, cln_b, pw2_w, pw2_b, w_out, ln_g, ln_b,
                *, batch, seq, mem_len, tt):
    tk = min(K_TILE, seq)
    zp, bg, q8, qi, wi, ka, vet, ki = _even_front(x, mem_kv, cosf, sinf, w_in, conv_w, conv_b, cln_g, cln_b,
                                                  pw2_w, pw2_b, w_out, batch=batch, seq=seq, mem_len=mem_len,
                                                  tt=tt, tk=tk)
    w_out_b = w_out[CONV_WIDTH:CONV_WIDTH + DSA_WIDTH].astype(BF16)
    return _dsa(q8, qi, wi, bg, zp, ka, vet, ki, w_out_b, ln_g, ln_b, batch=batch, seq=seq, tq=Q_TILE, tk=tk)


def kernel(x, mem, positions, e_w_in, e_conv_w, e_conv_b, e_cln_g, e_cln_b, e_pw2_w, e_pw2_b, e_w_out, o_w_in, o_vln_g, o_vln_b, o_ws, o_bs, o_w_out, mem_wk, mem_wv, ln_g, ln_b):
    batch, seq, d = x.shape
    mem_len = mem.shape[1]
    tt = min(TOKEN_TILE, seq)
    cosf, sinf = _rope_tables(positions)
    xf = x.reshape(batch * seq, d)
    memf = mem.reshape(batch * mem_len, d)
    for layer in range(DEPTH):
        j = layer // 2
        w_kv = jnp.concatenate([mem_wk[layer], mem_wv[layer]], axis=1).astype(BF16)
        mem_kv = _matmul(memf, w_kv, tm=min(tt, memf.shape[0]), tn=1024, out_dtype=BF16)
        if layer % 2 == 0:
            xf = _even_layer(xf, mem_kv, cosf, sinf, _even_w_in(e_w_in, j), e_conv_w[j], e_conv_b[j], e_cln_g[j], e_cln_b[j],
                             e_pw2_w[j], e_pw2_b[j], e_w_out[j], ln_g[layer], ln_b[layer],
                             batch=batch, seq=seq, mem_len=mem_len, tt=tt)
        else:
            xf = _odd_layer(xf, mem_kv, o_vln_g[j], o_vln_b[j], o_ws[j], o_bs[j], o_w_in[j], o_w_out[j],
                            ln_g[layer], ln_b[layer], seq=seq, mem_len=mem_len, tt=min(ODD_TOKEN_TILE, seq))
    return xf.reshape(batch, seq, d)
```

```python
import functools

import numpy as np
import jax
import jax.numpy as jnp
from jax import lax
from jax.experimental import pallas as pl
from jax.experimental.pallas import tpu as pltpu

F32 = jnp.float32
BF16 = jnp.bfloat16

D_MODEL = 1024
DEPTH = 4
HEAD_DIM = 64
CONV_WIDTH = 512
CONV_K = 31
DSA_HEADS = 8
DSA_WIDTH = 512
IDX_HEADS = 4
IDX_DIM = 64
TOPK_MAX = 256
SG_CHUNK = 128
SG_GROUPS = 8
SG_WIDTH = 1024
MEM_HEADS = 4
MEM_HEAD_DIM = 128
MEM_WIDTH = 512
ROPE_THETA = 10000.0
LN_EPS = 1e-5
DN_ALPHA = (2 * DEPTH) ** 0.25

LANES = 128
MXU_COLS = 256
SUBLANES = 8
VE_ROWS = HEAD_DIM + 16
CONV_HALO = 32
NEG = -1e30
INT_MIN = -(2 ** 31)
HALF16 = 2 ** 15
QK_SCALE_LOG2 = float(np.log2(np.e)) * HEAD_DIM ** -0.5
BOUND_MARGIN = 1.0 + 2.0 ** -8
Q_NORM_GUESS = QK_SCALE_LOG2 * HEAD_DIM ** 0.5
TINY_DENOM = 2.0 ** -100

VMEM_LIMIT = 56 * 1024 * 1024

TOKEN_TILE = 512
ODD_TOKEN_TILE = 1024
Q_TILE = 256
K_TILE = 256

_NT = (((1,), (1,)), ((), ()))


def _cparams(n_axes, flags=None):
    return pltpu.CompilerParams(dimension_semantics=("arbitrary",) * n_axes,
                                vmem_limit_bytes=VMEM_LIMIT, flags=flags)


def _resident(shape):
    zeros = (0,) * len(shape)
    return pl.BlockSpec(shape, lambda *_: zeros, pipeline_mode=pl.Buffered(1))


def _silu(g):
    return g * (1.0 / (1.0 + jnp.exp(-g)))


def _gelu_tanh(x):
    c = np.float32(np.sqrt(2.0 / np.pi))
    return x * (0.5 * (1.0 + jnp.tanh(c * (x + 0.044715 * (x * x * x)))))


def _layer_norm(x, g, b):
    mu = jnp.mean(x, axis=-1, keepdims=True)
    xc = x - mu
    var = jnp.mean(xc * xc, axis=-1, keepdims=True)
    return xc * lax.rsqrt(var + LN_EPS) * g + b


def _mem_heads(mq, kv_ref):
    outs = []
    for h in range(MEM_HEADS):
        hs = slice(MEM_HEAD_DIM * h, MEM_HEAD_DIM * (h + 1))
        vs = slice(MEM_WIDTH + MEM_HEAD_DIM * h, MEM_WIDTH + MEM_HEAD_DIM * (h + 1))
        s = lax.dot_general(mq[:, hs].astype(BF16), kv_ref[:, hs], _NT,
                            preferred_element_type=F32) * (MEM_HEAD_DIM ** -0.5)
        e = jnp.exp(s - jnp.max(s, axis=-1, keepdims=True))
        p = e / jnp.sum(e, axis=-1, keepdims=True)
        outs.append(jnp.dot(p.astype(BF16), kv_ref[:, vs], preferred_element_type=F32))
    return jnp.concatenate(outs, axis=1)


def _mm_kernel(x_ref, w_ref, o_ref):
    o_ref[...] = jnp.dot(x_ref[...].astype(BF16), w_ref[...],
                         preferred_element_type=F32).astype(o_ref.dtype)


def _matmul(x, w, *, tm, tn, out_dtype=F32):
    m, k = x.shape
    n = w.shape[1]
    return pl.pallas_call(
        _mm_kernel,
        name="mem_kv_proj",
        grid=(m // tm, n // tn),
        in_specs=[pl.BlockSpec((tm, k), lambda i, j: (i, 0)),
                  pl.BlockSpec((k, tn), lambda i, j: (0, j))],
        out_specs=pl.BlockSpec((tm, tn), lambda i, j: (i, j)),
        out_shape=jax.ShapeDtypeStruct((m, n), out_dtype),
        compiler_params=_cparams(2),
    )(x, w)


def _rope_table_kernel(pos_ref, inv_ref, cos_ref, sin_ref):
    ang = pos_ref[...].astype(F32) * inv_ref[...]
    half = HEAD_DIM // 2
    groups = LANES // half
    grp = lax.broadcasted_iota(jnp.int32, ang.shape, 1) // half
    sign = jnp.where((grp & 1) == 0, -1.0, 1.0)
    for table, out_ref, scale in ((jnp.cos(ang), cos_ref, None), (jnp.sin(ang), sin_ref, sign)):
        rolled = [table] + [pltpu.roll(table, half * k, 1) for k in range(1, groups)]
        for q in range(groups):
            y = rolled[(0 - q) % groups]
            for g in range(1, groups):
                y = jnp.where(grp == g, rolled[(g - q) % groups], y)
            out_ref[q] = y if scale is None else y * scale


def _rope_tables(positions):
    n = positions.size
    half = HEAD_DIM // 2
    per_row = LANES // half
    rows = n // per_row
    inv = (ROPE_THETA ** (-jnp.arange(0, HEAD_DIM, 2, dtype=F32) / HEAD_DIM)).astype(F32)
    pos_rep = jnp.repeat(jnp.transpose(positions.reshape(per_row, rows)), half, axis=1)
    inv_rep = jnp.tile(inv, per_row).reshape(1, LANES)
    tr = min(512, rows)
    cos, sin = pl.pallas_call(
        _rope_table_kernel,
        name="rope_tables",
        grid=(rows // tr,),
        in_specs=[pl.BlockSpec((tr, LANES), lambda i: (i, 0)),
                  pl.BlockSpec((1, LANES), lambda i: (0, 0))],
        out_specs=[pl.BlockSpec((per_row, tr, LANES), lambda i: (0, i, 0))] * 2,
        out_shape=[jax.ShapeDtypeStruct((per_row, rows, LANES), F32)] * 2,
        compiler_params=_cparams(1),
    )(pos_rep, inv_rep)
    return cos.reshape(n, LANES), sin.reshape(n, LANES)


_E_GATE_B = 3 * CONV_WIDTH
_E_MQ = _E_GATE_B + DSA_WIDTH
_E_MGATE = _E_MQ + MEM_WIDTH
_E_Q = _E_MGATE + MEM_WIDTH
_E_QI = _E_Q + DSA_WIDTH
_E_KKI = _E_QI + IDX_HEADS * IDX_DIM
_E_VW = _E_KKI + LANES
_E_TOTAL = _E_VW + LANES


_E_SPLIT_NAMES = ("a_val", "a_glu", "a_gate", "q", "k", "v", "qi", "ki", "wi", "b_gate", "mq", "m_gate")
_E_SPLIT_SIZES = (CONV_WIDTH, CONV_WIDTH, CONV_WIDTH, DSA_WIDTH, HEAD_DIM, HEAD_DIM,
                  IDX_HEADS * IDX_DIM, IDX_DIM, IDX_HEADS, DSA_WIDTH, MEM_WIDTH, MEM_WIDTH)
_E_SPLIT = dict(zip(_E_SPLIT_NAMES, zip(np.cumsum((0,) + _E_SPLIT_SIZES[:-1]).tolist(), _E_SPLIT_SIZES)))
_E_ORDER = ("a_val", "a_glu", "a_gate", "b_gate", "mq", "m_gate", "q", "qi", "k", "ki", "v", "wi")


def _even_w_layout_kernel(w_ref, o_ref):
    col = 0
    for name in _E_ORDER:
        src, width = _E_SPLIT[name]
        o_ref[:, col:col + width] = w_ref[:, src:src + width].astype(BF16)
        col += width
    o_ref[:, col:_E_TOTAL] = jnp.zeros((o_ref.shape[0], _E_TOTAL - col), BF16)


def _even_w_in(w_stack, layer):
    _, d, n_in = w_stack.shape
    tr = 256
    return pl.pallas_call(
        _even_w_layout_kernel,
        name="even_w_layout",
        grid=(d // tr,),
        in_specs=[pl.BlockSpec((None, tr, n_in), lambda i: (layer, i, 0))],
        out_specs=pl.BlockSpec((tr, _E_TOTAL), lambda i: (i, 0)),
        out_shape=jax.ShapeDtypeStruct((d, _E_TOTAL), BF16),
        compiler_params=_cparams(1),
    )(w_stack)


def _even_front_kernel(x_ref, win_ref, wout_ref, kv_ref, cw_ref, cb_ref, cg_ref, cbeta_ref, pw_ref, pb_ref,
                       cos_ref, sin_ref,
                       zp_ref, bg_ref, q8_ref, qio_ref, wi_ref, ka_ref, vet_ref, ki_ref,
                       hist_ref, shift_ref, *, tt, tk):
    @pl.when(pl.program_id(1) == 0)
    def _():
        hist_ref[0:CONV_HALO, :] = jnp.zeros((CONV_HALO, CONV_WIDTH), F32)

    x = x_ref[...]
    xb = x.astype(BF16)

    def proj(col, width):
        return jnp.dot(xb, win_ref[:, col:col + width], preferred_element_type=F32)

    glu = proj(CONV_WIDTH, CONV_WIDTH)
    hist_ref[CONV_HALO:CONV_HALO + tt, :] = proj(0, CONV_WIDTH) * (1.0 / (1.0 + jnp.exp(-glu)))
    span = tt + CONV_HALO - SUBLANES
    for r in range(1, SUBLANES):
        shift_ref[r - 1, 0:span, :] = hist_ref[pl.ds(r, span), :]
    base = CONV_HALO - (CONV_K - 1)
    acc = None
    for j in range(CONV_K):
        a, r = divmod(base + j, SUBLANES)
        rows = slice(SUBLANES * a, SUBLANES * a + tt)
        src = hist_ref[rows, :] if r == 0 else shift_ref[r - 1, rows, :]
        term = src * cw_ref[j:j + 1, :]
        acc = term if acc is None else acc + term
    acc = acc + cb_ref[...]
    hist_ref[0:CONV_HALO, :] = hist_ref[tt:tt + CONV_HALO, :]
    y = _silu(_layer_norm(acc, cg_ref[...], cbeta_ref[...]))
    y = jnp.dot(y.astype(BF16), pw_ref[...], preferred_element_type=F32) + pb_ref[...]
    ya = y * _silu(proj(2 * CONV_WIDTH, CONV_WIDTH))
    z = DN_ALPHA * x + jnp.dot(ya.astype(BF16), wout_ref[0:CONV_WIDTH, :], preferred_element_type=F32)

    ym = _mem_heads(proj(_E_MQ, MEM_WIDTH), kv_ref) * _silu(proj(_E_MGATE, MEM_WIDTH))
    m0 = CONV_WIDTH + DSA_WIDTH
    zp_ref[...] = z + jnp.dot(ym.astype(BF16), wout_ref[m0:m0 + MEM_WIDTH, :], preferred_element_type=F32)

    bg_ref[...] = proj(_E_GATE_B, DSA_WIDTH)
    cosf = cos_ref[...]
    sinf = sin_ref[...]

    lane = lax.broadcasted_iota(jnp.int32, (tt, LANES), 1)
    first_half = (lane & (HEAD_DIM - 1)) < HEAD_DIM // 2
    low = lane < HEAD_DIM
    marker = jnp.where(lane == HEAD_DIM, 1.0, 0.0)

    def rope(g):
        rot = jnp.where(first_half, pltpu.roll(g, LANES - HEAD_DIM // 2, 1), pltpu.roll(g, HEAD_DIM // 2, 1))
        return g * cosf + rot * sinf

    qf = proj(_E_Q, DSA_WIDTH)
    for g in range(DSA_WIDTH // LANES):
        sl = slice(LANES * g, LANES * (g + 1))
        q8_ref[:, sl] = (rope(qf[:, sl]) * QK_SCALE_LOG2).astype(BF16)
    qif = proj(_E_QI, IDX_HEADS * IDX_DIM)
    for g in range(IDX_HEADS * IDX_DIM // LANES):
        sl = slice(LANES * g, LANES * (g + 1))
        qio_ref[:, sl] = rope(qif[:, sl]).astype(BF16)
    kk = rope(proj(_E_KKI, LANES))
    ka_ref[...] = jnp.where(low, kk, marker).astype(BF16)
    ki_ref[...] = jnp.where(low, pltpu.roll(kk, HEAD_DIM, 1), 0.0).astype(BF16)
    vw = proj(_E_VW, LANES)
    ve = jnp.where(low, vw, marker)
    for c in range(tt // tk):
        vet_ref[c] = jnp.transpose(ve[tk * c:tk * (c + 1), :])[0:VE_ROWS, :].astype(BF16)
    wi_ref[...] = vw * ((IDX_HEADS ** -0.5) * (IDX_DIM ** -0.5))


def _even_front(x, mem_kv, cosf, sinf, w_in, conv_w, conv_b, cln_g, cln_b, pw_w, pw_b, w_out,
                *, batch, seq, mem_len, tt, tk):
    n = x.shape[0]
    nt = seq // tt
    wq = DSA_WIDTH
    wqi = IDX_HEADS * IDX_DIM
    cw = jnp.concatenate([conv_w, jnp.zeros((CONV_HALO - CONV_K, CONV_WIDTH), F32)], axis=0)
    vec = lambda v: v.reshape(1, -1)
    row = lambda b, i: (b * nt + i, 0)
    tile = lambda width: pl.BlockSpec((tt, width), row)
    return pl.pallas_call(
        functools.partial(_even_front_kernel, tt=tt, tk=tk),
        name="even_front",
        grid=(batch, nt),
        in_specs=[tile(D_MODEL),
                  _resident((D_MODEL, _E_TOTAL)),
                  _resident(w_out.shape),
                  pl.BlockSpec((mem_len, 2 * MEM_WIDTH), lambda b, i: (b, 0)),
                  _resident((CONV_HALO, CONV_WIDTH)),
                  _resident((1, CONV_WIDTH)), _resident((1, CONV_WIDTH)), _resident((1, CONV_WIDTH)),
                  _resident((CONV_WIDTH, CONV_WIDTH)),
                  _resident((1, CONV_WIDTH)),
                  tile(LANES), tile(LANES)],
        out_specs=[tile(D_MODEL), tile(DSA_WIDTH), tile(wq), tile(wqi), tile(LANES), tile(LANES),
                   pl.BlockSpec((tt // tk, VE_ROWS, tk), lambda b, i: (b * nt + i, 0, 0)),
                   tile(LANES)],
        out_shape=[jax.ShapeDtypeStruct((n, D_MODEL), F32), jax.ShapeDtypeStruct((n, DSA_WIDTH), F32),
                   jax.ShapeDtypeStruct((n, wq), BF16), jax.ShapeDtypeStruct((n, wqi), BF16),
                   jax.ShapeDtypeStruct((n, LANES), F32), jax.ShapeDtypeStruct((n, LANES), BF16),
                   jax.ShapeDtypeStruct((n // tk, VE_ROWS, tk), BF16), jax.ShapeDtypeStruct((n, LANES), BF16)],
        scratch_shapes=[pltpu.VMEM((CONV_HALO + tt, CONV_WIDTH), F32),
                        pltpu.VMEM((SUBLANES - 1, CONV_HALO + tt - SUBLANES, CONV_WIDTH), F32)],
        compiler_params=_cparams(2),
    )(x, w_in, w_out.astype(BF16), mem_kv, cw, vec(conv_b), vec(cln_g), vec(cln_b),
      pw_w.astype(BF16), vec(pw_b), cosf, sinf)


def _dsa_kernel(q8_ref, qi_ref, wi_ref, bg_ref, zp_ref, ka_ref, vet_ref, ki_ref, wout_ref, g_ref, b_ref, o_ref,
                keys_ref, hi_ref, lo_ref, bias_ref, s0_ref, s1_ref, p0_ref, p1_ref, acc_ref, m_ref,
                alpha_ref, knorm_ref, *, tq, tk, topk):
    i = pl.program_id(1)
    nkb = (i * tq + tq + tk - 1) // tk
    s_loc = lax.broadcasted_iota(jnp.int32, (tk, tq), 0)
    t_idx = i * tq + lax.broadcasted_iota(jnp.int32, (tk, tq), 1)

    low = lax.broadcasted_iota(jnp.int32, (tq, LANES), 1) < HEAD_DIM

    def split_heads(ref):
        heads = []
        for g in range(ref.shape[1] // LANES):
            two = ref[:, LANES * g:LANES * (g + 1)].astype(F32)
            heads += [jnp.where(low, two, 0.0), jnp.where(low, pltpu.roll(two, HEAD_DIM, 1), 0.0)]
        return heads

    qis = jnp.concatenate([qh.astype(BF16) for qh in split_heads(qi_ref)], axis=0)
    wt = jnp.transpose(wi_ref[...])[HEAD_DIM:HEAD_DIM + SUBLANES, :]

    npairs = (nkb + 1) // 2

    def score_block(j):
        kib = ki_ref[pl.ds(pl.multiple_of(j * tk, tk), tk), :]
        lg = lax.dot_general(kib, qis, _NT, preferred_element_type=F32)
        sc = jnp.maximum(lg[:, 0:tq], 0.0) * wt[0:1, :]
        for h in range(1, IDX_HEADS):
            sc = sc + jnp.maximum(lg[:, h * tq:(h + 1) * tq], 0.0) * wt[h:h + 1, :]
        bits = pltpu.bitcast(sc, jnp.int32)
        key = bits ^ ((bits >> 31) & 0x7FFFFFFF)
        key = jnp.where(j * tk + s_loc <= t_idx, key, INT_MIN)
        keys_ref[j] = key
        hi_ref[j] = (key >> 16).astype(jnp.int16)
        lo_ref[j] = ((key & 0xFFFF) - HALF16).astype(jnp.int16)

    def score_pair(t, carry):
        score_block(2 * t)
        score_block(2 * t + 1)
        return carry

    lax.fori_loop(0, npairs, score_pair, 0)

    nacc = tk // 4
    one, zero = jnp.asarray(1.0, BF16), jnp.asarray(0.0, BF16)

    def count16(d_ref, pred_fn):
        def chunks(j):
            r = jnp.where(pred_fn(d_ref[j]), one, zero).reshape(4, nacc, tq)
            return (r[0] + r[1]) + (r[2] + r[3])

        def pair(t, acc):
            return acc + (chunks(2 * t) + chunks(2 * t + 1))
        acc = lax.fori_loop(0, npairs, pair, jnp.zeros((nacc, tq), BF16))
        return jnp.sum(acc.astype(F32), axis=0, keepdims=True)

    def digit_search(d_ref, target):
        def bit_step(b, prefix):
            cand = prefix | lax.shift_left(jnp.int32(1), 15 - b)
            cand_s = (cand - HALF16).astype(jnp.int16)
            cnt = count16(d_ref, lambda d: d >= cand_s)
            return jnp.where(cnt >= target, cand, prefix)
        return lax.fori_loop(0, 16, bit_step, jnp.zeros((1, tq), jnp.int32))

    thr_hi = digit_search(hi_ref, topk) - HALF16
    thr_hi16 = thr_hi.astype(jnp.int16)
    c_hi = count16(hi_ref, lambda d: d > thr_hi16)

    def low_digit_pair(t, carry):
        for j in (2 * t, 2 * t + 1):
            lo_ref[j] = jnp.where(hi_ref[j] == thr_hi16, lo_ref[j], jnp.asarray(-HALF16, jnp.int16))
        return carry

    lax.fori_loop(0, npairs, low_digit_pair, 0)
    thr_lo = digit_search(lo_ref, topk - c_hi)
    thr_lo16 = (thr_lo - HALF16).astype(jnp.int16)
    c_gt = c_hi + count16(lo_ref, lambda d: d > thr_lo16)
    thr = lax.shift_left(thr_hi, 16) | thr_lo
    n_tie = jnp.where(thr == INT_MIN, 0.0, topk - c_gt)
    c_eq = count16(lo_ref, lambda d: d == thr_lo16)
    excess = jnp.max(jnp.where(thr == INT_MIN, 0.0, c_eq - n_tie))
    some_excess = excess > 0.5

    @pl.when(some_excess)
    def _():
        lower = jnp.where(lax.broadcasted_iota(jnp.int32, (tk, tk), 1)
                          < lax.broadcasted_iota(jnp.int32, (tk, tk), 0), 1.0, 0.0).astype(BF16)

        def select_block(j, off):
            kb = keys_ref[j]
            tie = kb == thr
            tie_f = jnp.where(tie, 1.0, 0.0)
            before = jnp.dot(lower, tie_f.astype(BF16), preferred_element_type=F32) + off
            bias_ref[j] = jnp.where(kb > thr, 0.0,
                                    jnp.where(tie, jnp.where(before < n_tie, 0.0, NEG), NEG))
            return off + jnp.sum(tie_f, axis=0, keepdims=True)

        def select_pair(t, off):
            return select_block(2 * t + 1, select_block(2 * t, off))

        lax.fori_loop(0, npairs, select_pair, jnp.zeros((1, tq), F32))

    @pl.when(jnp.logical_not(some_excess))
    def _():
        floor = jnp.where(thr == INT_MIN, INT_MIN + 1, thr)

        def select_pair(t, carry):
            for j in (2 * t, 2 * t + 1):
                bias_ref[j] = jnp.where(keys_ref[j] >= floor, 0.0, NEG)
            return carry

        lax.fori_loop(0, npairs, select_pair, 0)

    @pl.when(i == 0)
    def _():
        def block_norm(j, best):
            kf = ka_ref[pl.ds(pl.multiple_of(j * tk, tk), tk), :].astype(F32)
            return jnp.maximum(best, jnp.max(jnp.sum(kf * kf, axis=1, keepdims=True), axis=0, keepdims=True))
        k2 = lax.fori_loop(0, ka_ref.shape[0] // tk, block_norm, jnp.zeros((1, 1), F32))
        knorm_ref[...] = jnp.broadcast_to(k2, knorm_ref.shape)

    kmax = jnp.sqrt(knorm_ref[0:1, 0:1]) * BOUND_MARGIN
    slope = kmax * (-0.5 / Q_NORM_GUESS)
    offset = kmax * (-0.5 * Q_NORM_GUESS)
    shift_lane = lax.broadcasted_iota(jnp.int32, (tq, LANES), 1) == HEAD_DIM
    plain, shifted = [], []
    for qf in split_heads(q8_ref):
        neg_bound = jnp.sum(qf * qf, axis=1, keepdims=True) * slope + offset
        plain.append(qf.astype(BF16))
        shifted.append(jnp.where(shift_lane, neg_bound, qf).astype(BF16))
    qs_plain = jnp.concatenate(plain, axis=0)
    qs = jnp.concatenate(shifted, axis=0)
    acc_ref[...] = jnp.zeros(acc_ref.shape, F32)

    last = 2 * npairs - 1

    def masked_scores(q_stack, jc):
        rows = pl.ds(pl.multiple_of(jc * tk, tk), tk)
        s = lax.dot_general(ka_ref[rows, :], q_stack, _NT, preferred_element_type=F32)
        return [s[:, h * tq:(h + 1) * tq] + bias_ref[jc] for h in range(DSA_HEADS)]

    def scores_stage(s_ref, jb):
        for h, sh in enumerate(masked_scores(qs, jnp.minimum(jb, last))):
            s_ref[:, h * tq:(h + 1) * tq] = sh

    def exp_stage(s_ref, p_ref):
        for h in range(DSA_HEADS):
            hs = slice(h * tq, (h + 1) * tq)
            p_ref[:, hs] = jnp.exp2(s_ref[:, hs]).astype(BF16)

    def value_stage(p_ref, jb):
        acc_ref[...] += jnp.dot(vet_ref[jnp.clip(jb, 0, last)], p_ref[...], preferred_element_type=F32)

    scores_stage(s0_ref, 0)
    p1_ref[...] = jnp.zeros(p1_ref.shape, BF16)

    def attend_pair(t, carry):
        j = 2 * t
        scores_stage(s1_ref, j + 1)
        exp_stage(s0_ref, p0_ref)
        value_stage(p1_ref, j - 1)
        scores_stage(s0_ref, j + 2)
        exp_stage(s1_ref, p1_ref)
        value_stage(p0_ref, j)
        return carry

    lax.fori_loop(0, npairs, attend_pair, 0)
    value_stage(p1_ref, last)

    denom = acc_ref[HEAD_DIM:HEAD_DIM + 1, :]
    underflow = jnp.max(jnp.where(denom > TINY_DENOM, 0.0, 1.0)) > 0.5

    @pl.when(underflow)
    def _():
        m_ref[...] = jnp.full(m_ref.shape, NEG, F32)
        acc_ref[...] = jnp.zeros(acc_ref.shape, F32)

        def exact_block(j, carry):
            for h, sh in enumerate(masked_scores(qs_plain, j)):
                hs = slice(h * tq, (h + 1) * tq)
                m_old = m_ref[h:h + 1, :]
                m_new = jnp.maximum(m_old, jnp.max(sh, axis=0, keepdims=True))
                alpha_ref[h:h + 1, :] = jnp.exp2(m_old - m_new)
                m_ref[h:h + 1, :] = m_new
                p0_ref[:, hs] = jnp.exp2(sh - m_new).astype(BF16)
            pv = jnp.dot(vet_ref[j], p0_ref[...], preferred_element_type=F32)
            for h in range(DSA_HEADS):
                hs = slice(h * tq, (h + 1) * tq)
                acc_ref[:, hs] = acc_ref[:, hs] * alpha_ref[h:h + 1, :] + pv[:, hs]
            return carry

        lax.fori_loop(0, 2 * npairs, exact_block, 0)

    def head_out(h):
        a = acc_ref[:, h * tq:(h + 1) * tq]
        return a[0:HEAD_DIM, :] / a[HEAD_DIM:HEAD_DIM + 1, :]

    pairs = [jnp.transpose(jnp.concatenate([head_out(2 * g), head_out(2 * g + 1)], axis=0))
             for g in range(DSA_HEADS // 2)]
    yb = jnp.concatenate(pairs, axis=1) * _silu(bg_ref[...])
    z = zp_ref[...] + jnp.dot(yb.astype(BF16), wout_ref[...], preferred_element_type=F32)
    o_ref[...] = _layer_norm(z, g_ref[...], b_ref[...])


def _dsa(q8, qi, wi, bg, zp, ka, vet, ki, w_out_b, ln_g, ln_b, *, batch, seq, tq, tk):
    n = q8.shape[0]
    nq = seq // tq
    nk = seq // tk
    topk = min(TOPK_MAX, seq // 4)
    qrow = lambda b, i: (b * nq + i, 0)
    krow = lambda b, i: (b, 0)
    qtile = lambda width: pl.BlockSpec((tq, width), qrow)
    kernel = functools.partial(_dsa_kernel, tq=tq, tk=tk, topk=float(topk))
    return pl.pallas_call(
        kernel,
        name="dsa_attention",
        grid=(batch, nq),
        in_specs=[qtile(DSA_WIDTH), qtile(IDX_HEADS * IDX_DIM), qtile(LANES), qtile(DSA_WIDTH),
                  qtile(D_MODEL),
                  pl.BlockSpec((seq, LANES), krow),
                  pl.BlockSpec((nk, VE_ROWS, tk), lambda b, i: (b, 0, 0)),
                  pl.BlockSpec((seq, LANES), krow),
                  _resident((DSA_WIDTH, D_MODEL)), _resident((1, D_MODEL)), _resident((1, D_MODEL))],
        out_specs=qtile(D_MODEL),
        out_shape=jax.ShapeDtypeStruct((n, D_MODEL), F32),
        scratch_shapes=[pltpu.VMEM((nk, tk, tq), jnp.int32),
                        pltpu.VMEM((nk, tk, tq), jnp.int16),
                        pltpu.VMEM((nk, tk, tq), jnp.int16),
                        pltpu.VMEM((nk, tk, tq), F32),
                        pltpu.VMEM((tk, DSA_HEADS * tq), F32),
                        pltpu.VMEM((tk, DSA_HEADS * tq), F32),
                        pltpu.VMEM((tk, DSA_HEADS * tq), BF16),
                        pltpu.VMEM((tk, DSA_HEADS * tq), BF16),
                        pltpu.VMEM((VE_ROWS, DSA_HEADS * tq), F32),
                        pltpu.VMEM((DSA_HEADS, tq), F32),
                        pltpu.VMEM((DSA_HEADS, tq), F32),
                        pltpu.VMEM((SUBLANES, LANES), F32)],
        compiler_params=_cparams(2),
    )(q8, qi, wi, bg, zp, ka, vet, ki, w_out_b, ln_g.reshape(1, D_MODEL), ln_b.reshape(1, D_MODEL))


def _odd_kernel(x_ref, win_ref, wout_ref, kv_ref, vg_ref, vb_ref, ws_ref, bs_ref, g_ref, b_ref, o_ref,
                vn_ref, ug_ref, y_ref, *, tt):
    x = x_ref[...]
    xb = x.astype(BF16)

    def proj(col, width):
        return jnp.dot(xb, win_ref[:, col:col + width], preferred_element_type=F32)

    for c0 in range(0, SG_WIDTH, MXU_COLS):
        ug_ref[:, c0:c0 + MXU_COLS] = _gelu_tanh(proj(SG_WIDTH + c0, MXU_COLS))
    vn_ref[...] = _layer_norm(ug_ref[...], vg_ref[...], vb_ref[...]).astype(BF16)
    for c0 in range(0, SG_WIDTH, MXU_COLS):
        ug_ref[:, c0:c0 + MXU_COLS] = (_gelu_tanh(proj(c0, MXU_COLS))
                                       * _silu(proj(2 * SG_WIDTH + c0, MXU_COLS)))
    causal = (lax.broadcasted_iota(jnp.int32, (SG_CHUNK, SG_CHUNK), 1)
              <= lax.broadcasted_iota(jnp.int32, (SG_CHUNK, SG_CHUNK), 0))
    gw = SG_WIDTH // SG_GROUPS
    for g in range(SG_GROUPS):
        w = jnp.where(causal, ws_ref[g], 0.0).astype(BF16)
        bias = bs_ref[:, g:g + 1]
        cs = slice(gw * g, gw * (g + 1))
        for c in range(tt // SG_CHUNK):
            rs = slice(SG_CHUNK * c, SG_CHUNK * (c + 1))
            mixed = jnp.dot(w, vn_ref[rs, cs], preferred_element_type=F32) + bias
            y_ref[rs, cs] = (ug_ref[rs, cs] * mixed).astype(BF16)
    z = DN_ALPHA * x + jnp.dot(y_ref[...], wout_ref[0:SG_WIDTH, :], preferred_element_type=F32)
    ym = _mem_heads(proj(3 * SG_WIDTH, MEM_WIDTH), kv_ref) * _silu(proj(3 * SG_WIDTH + MEM_WIDTH, MEM_WIDTH))
    z = z + jnp.dot(ym.astype(BF16), wout_ref[SG_WIDTH:SG_WIDTH + MEM_WIDTH, :], preferred_element_type=F32)
    o_ref[...] = _layer_norm(z, g_ref[...], b_ref[...])


def _odd_layer(x, mem_kv, vln_g, vln_b, ws, bs, w_in, w_out, ln_g, ln_b, *, seq, mem_len, tt):
    n = x.shape[0]
    nt = seq // tt
    row = lambda i: (i, 0)
    vec = lambda v: v.reshape(1, -1)
    return pl.pallas_call(
        functools.partial(_odd_kernel, tt=tt),
        name="odd_layer",
        grid=(n // tt,),
        in_specs=[pl.BlockSpec((tt, D_MODEL), row),
                  _resident(w_in.shape),
                  _resident(w_out.shape),
                  pl.BlockSpec((mem_len, 2 * MEM_WIDTH), lambda i: (i // nt, 0)),
                  _resident((1, SG_WIDTH)), _resident((1, SG_WIDTH)),
                  _resident((SG_GROUPS, SG_CHUNK, SG_CHUNK)),
                  _resident((SG_CHUNK, SG_GROUPS)),
                  _resident((1, D_MODEL)), _resident((1, D_MODEL))],
        out_specs=pl.BlockSpec((tt, D_MODEL), row),
        out_shape=jax.ShapeDtypeStruct((n, D_MODEL), F32),
        scratch_shapes=[pltpu.VMEM((tt, SG_WIDTH), BF16),
                        pltpu.VMEM((tt, SG_WIDTH), F32),
                        pltpu.VMEM((tt, SG_WIDTH), BF16)],
        compiler_params=_cparams(1),
    )(x, w_in.astype(BF16), w_out.astype(BF16), mem_kv, vec(vln_g), vec(vln_b), ws, jnp.transpose(bs),
      vec(ln_g), vec(ln_b))


def _even_layer(x, mem_kv, cosf, sinf, w_in, conv_w, conv_b, cln_g, cln_b, pw2_w, pw2_b, w_out, ln_g, ln_b,
                *, batch, seq, mem_len, tt):
    tk = min(K_TILE, seq)
    zp, bg, q8, qi, wi, ka, vet, ki = _even_front(x, mem_kv, cosf, sinf, w_in, conv_w, conv_b, cln_g, cln_b,
                                                  pw2_w, pw2_b, w_out, batch=batch, seq=seq, mem_len=mem_len,
                                                  tt=tt, tk=tk)
    w_out_b = w_out[CONV_WIDTH:CONV_WIDTH + DSA_WIDTH].astype(BF16)
    return _dsa(q8, qi, wi, bg, zp, ka, vet, ki, w_out_b, ln_g, ln_b, batch=batch, seq=seq, tq=Q_TILE, tk=tk)


def kernel(x, mem, positions, e_w_in, e_conv_w, e_conv_b, e_cln_g, e_cln_b, e_pw2_w, e_pw2_b, e_w_out, o_w_in, o_vln_g, o_vln_b, o_ws, o_bs, o_w_out, mem_wk, mem_wv, ln_g, ln_b):
    batch, seq, d = x.shape
    mem_len = mem.shape[1]
    tt = min(TOKEN_TILE, seq)
    cosf, sinf = _rope_tables(positions)
    xf = x.reshape(batch * seq, d)
    memf = mem.reshape(batch * mem_len, d)
    for layer in range(DEPTH):
        j = layer // 2
        w_kv = jnp.concatenate([mem_wk[layer], mem_wv[layer]], axis=1).astype(BF16)
        mem_kv = _matmul(memf, w_kv, tm=min(tt, memf.shape[0]), tn=1024, out_dtype=BF16)
        if layer % 2 == 0:
            xf = _even_layer(xf, mem_kv, cosf, sinf, _even_w_in(e_w_in, j), e_conv_w[j], e_conv_b[j], e_cln_g[j], e_cln_b[j],
                             e_pw2_w[j], e_pw2_b[j], e_w_out[j], ln_g[layer], ln_b[layer],
                             batch=batch, seq=seq, mem_len=mem_len, tt=tt)
        else:
            xf = _odd_layer(xf, mem_kv, o_vln_g[j], o_vln_b[j], o_ws[j], o_bs[j], o_w_in[j], o_w_out[j],
                            ln_g[layer], ln_b[layer], seq=seq, mem_len=mem_len, tt=min(ODD_TOKEN_TILE, seq))
    return xf.reshape(batch, seq, d)
```
